```python
import math, functools
import jax, jax.numpy as jnp
from jax import lax
import numpy as np

D_MODEL = 2048
BATCH = 8
SEQ = 8192
DEPTH = 4

GRID_W = 64
CTX_LEN = 256
HEAD_DIM = 128
RET_HEADS = 4
DN_HEADS = 4
ATT_HEADS = 8
ATT_KV_HEADS = 2
RET_W = RET_HEADS * HEAD_DIM
DN_W = DN_HEADS * HEAD_DIM
ATT_W = ATT_HEADS * HEAD_DIM
ATT_KV_W = ATT_KV_HEADS * HEAD_DIM
MIX_W = RET_W + DN_W + ATT_W
RET_CHUNK = 128
DN_CHUNK = 64
DN_CONV_K = 5
Q_BLOCK = 128
ROPE_THETA = 10000.0
D_FF = ((8 * D_MODEL + 3 * 256 - 1) // (3 * 256)) * 256
DEEPNORM_ALPHA = (2 * DEPTH) ** 0.25
DEEPNORM_BETA = (8 * DEPTH) ** -0.25
EPS = 1e-6
SPLIT_SIZES = (RET_W, RET_W, RET_W, RET_W, 3 * DN_W, DN_W, 2 * DN_HEADS, 2 * DN_HEADS, ATT_W, ATT_KV_W, ATT_KV_W)
PROJ_W = sum(SPLIT_SIZES)
SPLIT_POINTS = tuple(np.cumsum(SPLIT_SIZES)[:-1].tolist())

kernel_name = "hybrid_ret_gdn_gqa_diffusion_block"


def layer_norm(x, w, b):
    xf = x.astype(jnp.float32)
    mu = jnp.mean(xf, -1, keepdims=True)
    var = jnp.mean(jnp.square(xf - mu), -1, keepdims=True)
    return (xf - mu) * lax.rsqrt(var + EPS) * w + b


def rms_norm(x, w=None):
    xf = x.astype(jnp.float32)
    y = xf * lax.rsqrt(jnp.mean(xf * xf, -1, keepdims=True) + EPS)
    if w is not None:
        y = y * w
    return y.astype(x.dtype)


def l2_normalize(x):
    return x * lax.rsqrt(jnp.sum(x * x, -1, keepdims=True) + EPS)


def split_heads(a, n_heads):
    return a.reshape(a.shape[:-1] + (n_heads, HEAD_DIM))


def modulate(h, shift, scale):
    return h * (1.0 + scale) + shift


def post_norm(x, y, w, b):
    return layer_norm(DEEPNORM_ALPHA * x + y, w, b).astype(x.dtype)


def axial_rope(n_tokens):
    rows = n_tokens // GRID_W
    row = jnp.repeat(jnp.arange(rows, dtype=jnp.float32), GRID_W)
    col = jnp.tile(jnp.arange(GRID_W, dtype=jnp.float32), rows)
    n_freq = HEAD_DIM // 4
    inv = ROPE_THETA ** (-jnp.arange(n_freq, dtype=jnp.float32) / n_freq)
    ang = jnp.concatenate([row[:, None] * inv, col[:, None] * inv], -1)
    return jnp.cos(ang), jnp.sin(ang)


def apply_rope(x, cos, sin):
    xf = x.astype(jnp.float32)
    x1, x2 = jnp.split(xf, 2, -1)
    c = cos[None, :, None, :]
    s = sin[None, :, None, :]
    return jnp.concatenate([x1 * c - x2 * s, x1 * s + x2 * c], -1).astype(x.dtype)


def bidirectional(scan_f, scan_b, ctx_f, lat_f, ctx_b, lat_b, s0):
    flip = lambda seq: tuple(jnp.flip(a, axis=1) for a in seq)
    o_cf, s_cf = scan_f(*ctx_f, s0)
    o_lf, _ = scan_f(*lat_f, s_cf)
    o_cb, s_cb = scan_b(*flip(ctx_b), s0)
    o_lb, _ = scan_b(*flip(lat_b), s_cb)
    return o_cf + jnp.flip(o_cb, 1), o_lf + jnp.flip(o_lb, 1)


def retention_scan(q, k, v, s0, log_gamma):
    b, l, h, d = q.shape
    n = l // RET_CHUNK
    qc = q.reshape(b, n, RET_CHUNK, h, d)
    kc = k.reshape(b, n, RET_CHUNK, h, d)
    vc = v.reshape(b, n, RET_CHUNK, h, d)
    pos = jnp.arange(RET_CHUNK, dtype=jnp.float32)
    rel = pos[:, None] - pos[None, :]
    decay = jnp.where(rel >= 0, jnp.exp(jnp.maximum(rel, 0.0)[None] * log_gamma[:, None, None]), 0.0)
    intra = jnp.einsum('bnihd,bnjhd->bnhij', qc, kc) * decay
    o_intra = jnp.einsum('bnhij,bnjhd->bnihd', intra, vc)
    q_decay = jnp.exp((pos + 1.0)[:, None] * log_gamma[None, :])
    k_decay = jnp.exp((RET_CHUNK - 1.0 - pos)[:, None] * log_gamma[None, :])
    chunk_kv = jnp.einsum('bnjhd,jh,bnjhe->nbhde', kc, k_decay, vc)
    chunk_decay = jnp.exp(RET_CHUNK * log_gamma)[None, :, None, None]

    def step(s, u):
        return s * chunk_decay + u, s

    s_fin, s_prev = lax.scan(step, s0, chunk_kv)
    o_inter = jnp.einsum('bnihd,ih,nbhde->bnihe', qc, q_decay, s_prev)
    return (o_intra + o_inter).reshape(b, l, h, d), s_fin


def to_chunks(a, c):
    b, l = a.shape[:2]
    return jnp.swapaxes(a.reshape((b, l // c, c) + a.shape[2:]), 2, 3)


def gated_delta_scan(q, k, v, g, beta, s0):
    b, l, h, _ = q.shape
    c = DN_CHUNK
    qc, kc, vc = to_chunks(q, c), to_chunks(k, c), to_chunks(v, c)
    gc, bc = to_chunks(g, c), to_chunks(beta, c)
    g_cum = jnp.cumsum(gc, -1)
    tri = jnp.tril(jnp.ones((c, c), bool))
    strict = jnp.tril(jnp.ones((c, c), bool), -1)
    diff = g_cum[..., :, None] - g_cum[..., None, :]
    decay = jnp.where(tri, jnp.exp(jnp.where(tri, diff, 0.0)), 0.0)
    k_beta = kc * bc[..., None]
    v_beta = vc * bc[..., None]
    a = jnp.where(strict, jnp.einsum('bnhid,bnhjd->bnhij', k_beta, kc) * decay, 0.0)
    eye = jnp.eye(c, dtype=a.dtype)
    t = lax.linalg.triangular_solve(eye + a, jnp.broadcast_to(eye, a.shape), left_side=True,
                                    lower=True, unit_diagonal=True)
    w_val = jnp.einsum('bnhij,bnhjd->bnhid', t, v_beta)
    k_cum = jnp.einsum('bnhij,bnhjd->bnhid', t, k_beta * jnp.exp(g_cum)[..., None])
    qk = jnp.einsum('bnhid,bnhjd->bnhij', qc, kc) * decay
    q_g = qc * jnp.exp(g_cum)[..., None]
    k_g = kc * jnp.exp(g_cum[..., -1:] - g_cum)[..., None]
    g_last = jnp.exp(g_cum[..., -1])
    xs = tuple(jnp.moveaxis(z, 1, 0) for z in (w_val, k_cum, qk, q_g, k_g, g_last))

    def step(s, inp):
        w_i, kc_i, qk_i, qg_i, kg_i, gl_i = inp
        v_new = w_i - jnp.einsum('bhcd,bhde->bhce', kc_i, s)
        o = jnp.einsum('bhcd,bhde->bhce', qg_i, s) + jnp.einsum('bhij,bhje->bhie', qk_i, v_new)
        s = s * gl_i[..., None, None] + jnp.einsum('bhcd,bhce->bhde', kg_i, v_new)
        return s, o

    s_fin, o = lax.scan(step, s0, xs)
    return o.transpose(1, 0, 3, 2, 4).reshape(b, l, h, -1), s_fin


def short_conv(x, w):
    pad = DN_CONV_K // 2
    return lax.conv_general_dilated(x, w[:, None, :], window_strides=(1,), padding=[(pad, pad)],
                                    dimension_numbers=('NWC', 'WIO', 'NWC'),
                                    feature_group_count=x.shape[-1])


def block_attention(q, k, v):
    b, lq, h, d = q.shape
    kvh = k.shape[2]
    qb = q.reshape(b, lq // Q_BLOCK, Q_BLOCK, kvh, h // kvh, d).swapaxes(0, 1)

    def one_block(qi):
        s = jnp.einsum('bqhgd,bkhd->bhgqk', qi, k).astype(jnp.float32) * d ** -0.5
        p = jax.nn.softmax(s, axis=-1).astype(v.dtype)
        return jnp.einsum('bhgqk,bkhd->bqhgd', p, v)

    o = lax.map(one_block, qb)
    return o.swapaxes(0, 1).reshape(b, lq, h * d)


def retention_group(pc, pl, decay_logit, cos, sin):
    log_gamma = jax.nn.log_sigmoid(decay_logit.astype(jnp.float32))

    def qkv(p, rotate):
        q, k, v = (split_heads(a, RET_HEADS).astype(jnp.float32) for a in p[:3])
        if rotate:
            q, k = apply_rope(q, cos, sin), apply_rope(k, cos, sin)
        return q, k * HEAD_DIM ** -0.5, v

    ctx_seq = qkv(pc, False)
    lat_seq = qkv(pl, True)
    s0 = jnp.zeros((pc[0].shape[0], RET_HEADS, HEAD_DIM, HEAD_DIM), jnp.float32)
    o_c, o_l = bidirectional(functools.partial(retention_scan, log_gamma=log_gamma[0]),
                             functools.partial(retention_scan, log_gamma=log_gamma[1]),
                             ctx_seq, lat_seq, ctx_seq, lat_seq, s0)

    def out(o, g):
        y = rms_norm(o) * jax.nn.silu(split_heads(g, RET_HEADS).astype(jnp.float32))
        return y.reshape(y.shape[:2] + (RET_W,)).astype(g.dtype)

    return out(o_c, pc[3]), out(o_l, pl[3])


def deltanet_group(pc, pl, conv_w, a_log, dt_bias, norm_w):
    neg_a = -jnp.exp(a_log.astype(jnp.float32))
    dt_b = dt_bias.astype(jnp.float32)

    def prep(p):
        qkv, _, a, bb = p
        qkv = jax.nn.silu(short_conv(qkv, conv_w.astype(qkv.dtype)))
        q, k, v = (split_heads(t, DN_HEADS).astype(jnp.float32) for t in jnp.split(qkv, 3, -1))
        q = l2_normalize(q) * HEAD_DIM ** -0.5
        k = l2_normalize(k)
        bsz, n = a.shape[:2]
        g = neg_a * jax.nn.softplus(a.astype(jnp.float32).reshape(bsz, n, 2, DN_HEADS) + dt_b)
        beta = jax.nn.sigmoid(bb.astype(jnp.float32).reshape(bsz, n, 2, DN_HEADS))
        return (q, k, v, g[:, :, 0], beta[:, :, 0]), (q, k, v, g[:, :, 1], beta[:, :, 1])

    cf, cb = prep(pc)
    lf, lb = prep(pl)
    s0 = jnp.zeros((pc[0].shape[0], DN_HEADS, HEAD_DIM, HEAD_DIM), jnp.float32)
    o_c, o_l = bidirectional(gated_delta_scan, gated_delta_scan, cf, lf, cb, lb, s0)

    def out(o, z):
        y = rms_norm(o, norm_w) * jax.nn.silu(split_heads(z, DN_HEADS).astype(jnp.float32))
        return y.reshape(y.shape[:2] + (DN_W,)).astype(z.dtype)

    return out(o_c, pc[1]), out(o_l, pl[1])


def attention_group(pc, pl, qn_w, kn_w, cos, sin, keep_ctx):
    def qkv(p):
        q = rms_norm(split_heads(p[0], ATT_HEADS), qn_w)
        k = rms_norm(split_heads(p[1], ATT_KV_HEADS), kn_w)
        return q, k, split_heads(p[2], ATT_KV_HEADS)

    qc, kc, vc = qkv(pc)
    ql, kl, vl = qkv(pl)
    ql, kl = apply_rope(ql, cos, sin), apply_rope(kl, cos, sin)
    y_l = block_attention(ql, jnp.concatenate([kc, kl], 1), jnp.concatenate([vc, vl], 1))
    y_c = block_attention(qc, kc, vc) if keep_ctx else None
    return y_c, y_l


def hybrid_mixer(h_ctx, h_lat, w_in, ret_decay_logit, dn_conv_w, dn_a_log, dn_dt_bias, dn_norm_w,
                 att_qn_w, att_kn_w, cos, sin, keep_ctx):
    pc = jnp.split(h_ctx @ w_in, SPLIT_POINTS, axis=-1)
    pl = jnp.split(h_lat @ w_in, SPLIT_POINTS, axis=-1)
    rc, rl = retention_group(pc[0:4], pl[0:4], ret_decay_logit, cos, sin)
    dc, dl = deltanet_group(pc[4:8], pl[4:8], dn_conv_w, dn_a_log, dn_dt_bias, dn_norm_w)
    ac, al = attention_group(pc[8:11], pl[8:11], att_qn_w, att_kn_w, cos, sin, keep_ctx)
    y_lat = jnp.concatenate([rl, dl, al], -1)
    y_ctx = jnp.concatenate([rc, dc, ac], -1) if keep_ctx else None
    return y_ctx, y_lat


def swiglu(h, w_in, w_out):
    gate, up = jnp.split(h @ w_in, 2, -1)
    return (jax.nn.silu(gate) * up) @ w_out


def _fwd_setup_inputs(seed: int = 0) -> dict:
    key = jax.random.key(seed)
    ks = jax.random.split(key, 24)
    f32 = jnp.float32

    def nrm(k, shape, scale):
        return jax.random.normal(k, shape, f32) * scale

    base_logit = jnp.log(2.0 ** (5.0 + jnp.arange(RET_HEADS, dtype=f32)) - 1.0)
    dt = jnp.exp(jax.random.uniform(ks[10], (DEPTH, 2, DN_HEADS), f32, math.log(1e-3), math.log(1e-1)))
    return {
        "x": nrm(ks[0], (BATCH, SEQ, D_MODEL), 1.0),
        "c": nrm(ks[1], (BATCH, D_MODEL), 1.0),
        "ctx": nrm(ks[2], (BATCH, CTX_LEN, D_MODEL), 1.0),
        "c_ctx": nrm(ks[3], (D_MODEL,), 1.0),
        "w_ada": nrm(ks[4], (DEPTH, D_MODEL, 6 * D_MODEL), 0.5 * D_MODEL ** -0.5),
        "b_ada": nrm(ks[5], (DEPTH, 6 * D_MODEL), 0.02),
        "w_in": nrm(ks[6], (DEPTH, D_MODEL, PROJ_W), D_MODEL ** -0.5),
        "ret_decay_logit": base_logit + nrm(ks[7], (DEPTH, 2, RET_HEADS), 0.1),
        "dn_conv_w": nrm(ks[8], (DEPTH, DN_CONV_K, 3 * DN_W), DN_CONV_K ** -0.5),
        "dn_a_log": jnp.log(jax.random.uniform(ks[9], (DEPTH, 2, DN_HEADS), f32, 1.0, 16.0)),
        "dn_dt_bias": dt + jnp.log(-jnp.expm1(-dt)),
        "dn_norm_w": 1.0 + nrm(ks[11], (DEPTH, HEAD_DIM), 0.02),
        "att_qn_w": 1.0 + nrm(ks[12], (DEPTH, HEAD_DIM), 0.02),
        "att_kn_w": 1.0 + nrm(ks[13], (DEPTH, HEAD_DIM), 0.02),
        "w_o": nrm(ks[14], (DEPTH, MIX_W, D_MODEL), MIX_W ** -0.5 * DEEPNORM_BETA),
        "ln1_w": 1.0 + nrm(ks[15], (DEPTH, D_MODEL), 0.02),
        "ln1_b": nrm(ks[16], (DEPTH, D_MODEL), 0.02),
        "w_ffn_in": nrm(ks[17], (DEPTH, D_MODEL, 2 * D_FF), D_MODEL ** -0.5),
        "w_ffn_out": nrm(ks[18], (DEPTH, D_FF, D_MODEL), D_FF ** -0.5 * DEEPNORM_BETA),
        "ln2_w": 1.0 + nrm(ks[19], (DEPTH, D_MODEL), 0.02),
        "ln2_b": nrm(ks[20], (DEPTH, D_MODEL), 0.02),
    }


def _fwd_reference(x, c, ctx, c_ctx, w_ada, b_ada, w_in, ret_decay_logit, dn_conv_w, dn_a_log, dn_dt_bias,
              dn_norm_w, att_qn_w, att_kn_w, w_o, ln1_w, ln1_b, w_ffn_in, w_ffn_out, ln2_w, ln2_b):
    cos, sin = axial_rope(x.shape[1])
    cond_lat = jax.nn.silu(c)
    cond_ctx = jax.nn.silu(c_ctx)
    for i in range(DEPTH):
        keep_ctx = i < DEPTH - 1
        m_l = jnp.split((cond_lat @ w_ada[i] + b_ada[i])[:, None, :], 6, -1)
        m_c = jnp.split((cond_ctx @ w_ada[i] + b_ada[i])[None, None, :], 6, -1)
        y_c, y_l = hybrid_mixer(modulate(ctx, m_c[0], m_c[1]), modulate(x, m_l[0], m_l[1]), w_in[i],
                                ret_decay_logit[i], dn_conv_w[i], dn_a_log[i], dn_dt_bias[i], dn_norm_w[i],
                                att_qn_w[i], att_kn_w[i], cos, sin, keep_ctx)
        x = post_norm(x, m_l[2] * (y_l @ w_o[i]), ln1_w[i], ln1_b[i])
        x = post_norm(x, m_l[5] * swiglu(modulate(x, m_l[3], m_l[4]), w_ffn_in[i], w_ffn_out[i]),
                      ln2_w[i], ln2_b[i])
        if keep_ctx:
            ctx = post_norm(ctx, m_c[2] * (y_c @ w_o[i]), ln1_w[i], ln1_b[i])
            ctx = post_norm(ctx, m_c[5] * swiglu(modulate(ctx, m_c[3], m_c[4]), w_ffn_in[i], w_ffn_out[i]),
                            ln2_w[i], ln2_b[i])
    return x


import jax as _jax
import jax.numpy as _jnp

TWIN_FORMAT = 'train_step'
FWD_PARAMS = ['x', 'c', 'ctx', 'c_ctx', 'w_ada', 'b_ada', 'w_in', 'ret_decay_logit', 'dn_conv_w', 'dn_a_log', 'dn_dt_bias', 'dn_norm_w', 'att_qn_w', 'att_kn_w', 'w_o', 'ln1_w', 'ln1_b', 'w_ffn_in', 'w_ffn_out', 'ln2_w', 'ln2_b']
TWIN_WEIGHTS = ['c_ctx', 'w_ada', 'b_ada', 'w_in', 'ret_decay_logit', 'dn_conv_w', 'dn_a_log', 'dn_dt_bias', 'dn_norm_w', 'att_qn_w', 'att_kn_w', 'w_o', 'ln1_w', 'ln1_b', 'w_ffn_in', 'w_ffn_out', 'ln2_w', 'ln2_b']
TWIN_DIFF_INPUT = 'x'
TWIN_INPUTS = ['x', 'c', 'ctx', 'c_ctx', 'w_ada', 'b_ada', 'w_in', 'ret_decay_logit', 'dn_conv_w', 'dn_a_log', 'dn_dt_bias', 'dn_norm_w', 'att_qn_w', 'att_kn_w', 'w_o', 'ln1_w', 'ln1_b', 'w_ffn_in', 'w_ffn_out', 'ln2_w', 'ln2_b', 'loss_target', 'm_c_ctx', 'm_w_ada', 'm_b_ada', 'm_w_in', 'm_ret_decay_logit', 'm_dn_conv_w', 'm_dn_a_log', 'm_dn_dt_bias', 'm_dn_norm_w', 'm_att_qn_w', 'm_att_kn_w', 'm_w_o', 'm_ln1_w', 'm_ln1_b', 'm_w_ffn_in', 'm_w_ffn_out', 'm_ln2_w', 'm_ln2_b', 'v_c_ctx', 'v_w_ada', 'v_b_ada', 'v_w_in', 'v_ret_decay_logit', 'v_dn_conv_w', 'v_dn_a_log', 'v_dn_dt_bias', 'v_dn_norm_w', 'v_att_qn_w', 'v_att_kn_w', 'v_w_o', 'v_ln1_w', 'v_ln1_b', 'v_w_ffn_in', 'v_w_ffn_out', 'v_ln2_w', 'v_ln2_b']
TWIN_OUTPUTS = ['loss', 'grad_x', 'grad_c_ctx', 'grad_w_ada', 'grad_b_ada', 'grad_w_in', 'grad_ret_decay_logit', 'grad_dn_conv_w', 'grad_dn_a_log', 'grad_dn_dt_bias', 'grad_dn_norm_w', 'grad_att_qn_w', 'grad_att_kn_w', 'grad_w_o', 'grad_ln1_w', 'grad_ln1_b', 'grad_w_ffn_in', 'grad_w_ffn_out', 'grad_ln2_w', 'grad_ln2_b', 'delta_c_ctx', 'delta_w_ada', 'delta_b_ada', 'delta_w_in', 'delta_ret_decay_logit', 'delta_dn_conv_w', 'delta_dn_a_log', 'delta_dn_dt_bias', 'delta_dn_norm_w', 'delta_att_qn_w', 'delta_att_kn_w', 'delta_w_o', 'delta_ln1_w', 'delta_ln1_b', 'delta_w_ffn_in', 'delta_w_ffn_out', 'delta_ln2_w', 'delta_ln2_b', 'new_m_c_ctx', 'new_m_w_ada', 'new_m_b_ada', 'new_m_w_in', 'new_m_ret_decay_logit', 'new_m_dn_conv_w', 'new_m_dn_a_log', 'new_m_dn_dt_bias', 'new_m_dn_norm_w', 'new_m_att_qn_w', 'new_m_att_kn_w', 'new_m_w_o', 'new_m_ln1_w', 'new_m_ln1_b', 'new_m_w_ffn_in', 'new_m_w_ffn_out', 'new_m_ln2_w', 'new_m_ln2_b', 'new_v_c_ctx', 'new_v_w_ada', 'new_v_b_ada', 'new_v_w_in', 'new_v_ret_decay_logit', 'new_v_dn_conv_w', 'new_v_dn_a_log', 'new_v_dn_dt_bias', 'new_v_dn_norm_w', 'new_v_att_qn_w', 'new_v_att_kn_w', 'new_v_w_o', 'new_v_ln1_w', 'new_v_ln1_b', 'new_v_w_ffn_in', 'new_v_w_ffn_out', 'new_v_ln2_w', 'new_v_ln2_b']
TWIN_LEAF_KINDS = {'loss': 'loss', 'grad_x': 'grad_x', 'grad_c_ctx': 'grad_w', 'grad_w_ada': 'grad_w', 'grad_b_ada': 'grad_w', 'grad_w_in': 'grad_w', 'grad_ret_decay_logit': 'grad_w', 'grad_dn_conv_w': 'grad_w', 'grad_dn_a_log': 'grad_w', 'grad_dn_dt_bias': 'grad_w', 'grad_dn_norm_w': 'grad_w', 'grad_att_qn_w': 'grad_w', 'grad_att_kn_w': 'grad_w', 'grad_w_o': 'grad_w', 'grad_ln1_w': 'grad_w', 'grad_ln1_b': 'grad_w', 'grad_w_ffn_in': 'grad_w', 'grad_w_ffn_out': 'grad_w', 'grad_ln2_w': 'grad_w', 'grad_ln2_b': 'grad_w', 'delta_c_ctx': 'delta_w', 'delta_w_ada': 'delta_w', 'delta_b_ada': 'delta_w', 'delta_w_in': 'delta_w', 'delta_ret_decay_logit': 'delta_w', 'delta_dn_conv_w': 'delta_w', 'delta_dn_a_log': 'delta_w', 'delta_dn_dt_bias': 'delta_w', 'delta_dn_norm_w': 'delta_w', 'delta_att_qn_w': 'delta_w', 'delta_att_kn_w': 'delta_w', 'delta_w_o': 'delta_w', 'delta_ln1_w': 'delta_w', 'delta_ln1_b': 'delta_w', 'delta_w_ffn_in': 'delta_w', 'delta_w_ffn_out': 'delta_w', 'delta_ln2_w': 'delta_w', 'delta_ln2_b': 'delta_w', 'new_m_c_ctx': 'new_m', 'new_m_w_ada': 'new_m', 'new_m_b_ada': 'new_m', 'new_m_w_in': 'new_m', 'new_m_ret_decay_logit': 'new_m', 'new_m_dn_conv_w': 'new_m', 'new_m_dn_a_log': 'new_m', 'new_m_dn_dt_bias': 'new_m', 'new_m_dn_norm_w': 'new_m', 'new_m_att_qn_w': 'new_m', 'new_m_att_kn_w': 'new_m', 'new_m_w_o': 'new_m', 'new_m_ln1_w': 'new_m', 'new_m_ln1_b': 'new_m', 'new_m_w_ffn_in': 'new_m', 'new_m_w_ffn_out': 'new_m', 'new_m_ln2_w': 'new_m', 'new_m_ln2_b': 'new_m', 'new_v_c_ctx': 'new_v', 'new_v_w_ada': 'new_v', 'new_v_b_ada': 'new_v', 'new_v_w_in': 'new_v', 'new_v_ret_decay_logit': 'new_v', 'new_v_dn_conv_w': 'new_v', 'new_v_dn_a_log': 'new_v', 'new_v_dn_dt_bias': 'new_v', 'new_v_dn_norm_w': 'new_v', 'new_v_att_qn_w': 'new_v', 'new_v_att_kn_w': 'new_v', 'new_v_w_o': 'new_v', 'new_v_ln1_w': 'new_v', 'new_v_ln1_b': 'new_v', 'new_v_w_ffn_in': 'new_v', 'new_v_w_ffn_out': 'new_v', 'new_v_ln2_w': 'new_v', 'new_v_ln2_b': 'new_v'}


def _forward(args):
    return _fwd_reference(*[args[k] for k in FWD_PARAMS])


def _output_shape():
    def fwd():
        inp = _fwd_setup_inputs(0)
        return _fwd_reference(*[inp[k] for k in FWD_PARAMS])
    out = _jax.eval_shape(fwd)
    return out.shape, out.dtype

N_MICROBATCH = 1
ADAM_LR = 0.001
ADAM_B1 = 0.9
ADAM_B2 = 0.999
ADAM_EPS = 1e-08
ADAM_WD = 0.01
ADAM_STEP = 10
PER_EXAMPLE_BATCH_AXIS = {'x': 0, 'c': 0, 'ctx': 0, 'loss_target': 0}
SHARED_INPUTS = []
_WEIGHT_DTYPES = {'c_ctx': _jnp.float32, 'w_ada': _jnp.float32, 'b_ada': _jnp.float32, 'w_in': _jnp.float32, 'ret_decay_logit': _jnp.float32, 'dn_conv_w': _jnp.float32, 'dn_a_log': _jnp.float32, 'dn_dt_bias': _jnp.float32, 'dn_norm_w': _jnp.float32, 'att_qn_w': _jnp.float32, 'att_kn_w': _jnp.float32, 'w_o': _jnp.float32, 'ln1_w': _jnp.float32, 'ln1_b': _jnp.float32, 'w_ffn_in': _jnp.float32, 'w_ffn_out': _jnp.float32, 'ln2_w': _jnp.float32, 'ln2_b': _jnp.float32}
MOMENT_SCALE = {'c_ctx': 4.224244e-03, 'w_ada': 8.813865e-03, 'b_ada': 1.556534e-02, 'w_in': 5.344381e-03, 'ret_decay_logit': 2.012984e-02, 'dn_conv_w': 4.432285e-03, 'dn_a_log': 1.345345e-02, 'dn_dt_bias': 1.299636e-02, 'dn_norm_w': 1.334056e-02, 'att_qn_w': 2.223841e-03, 'att_kn_w': 2.280553e-03, 'w_o': 1.256980e-02, 'ln1_w': 1.136481e+00, 'ln1_b': 5.194139e-01, 'w_ffn_in': 4.122652e-03, 'w_ffn_out': 1.603288e-02, 'ln2_w': 1.608292e+01, 'ln2_b': 9.554371e-01}


def _to_microbatches(a, axis):
    t = _jnp.moveaxis(a, axis, 0)
    t = t.reshape((N_MICROBATCH, t.shape[0] // N_MICROBATCH) + t.shape[1:])
    return _jnp.moveaxis(t, 1, axis + 1)


def setup_inputs(seed: int = 0) -> dict:
    inp = _fwd_setup_inputs(seed)
    key = _jax.random.fold_in(_jax.random.key(seed), 7919)
    shape, _ = _output_shape()
    out = dict(inp)
    out["loss_target"] = _jax.random.normal(_jax.random.fold_in(key, 0), shape, _jnp.float32)
    for i, name in enumerate(TWIN_WEIGHTS):
        w = inp[name].astype(_jnp.float32)
        if MOMENT_SCALE is None:
            s = _jnp.sqrt(_jnp.mean(_jnp.square(w)) + 1e-30)
        else:
            s = MOMENT_SCALE[name]
        km, kv = _jax.random.split(_jax.random.fold_in(key, i + 1))
        out[name] = w
        out["m_" + name] = s * _jax.random.normal(km, w.shape, _jnp.float32)
        out["v_" + name] = (s * s) * _jax.random.uniform(kv, w.shape, _jnp.float32, 0.5, 1.5)
    if N_MICROBATCH > 1:
        for name, axis in PER_EXAMPLE_BATCH_AXIS.items():
            out[name] = _to_microbatches(out[name], axis)
    return {'x': out['x'], 'c': out['c'], 'ctx': out['ctx'], 'c_ctx': out['c_ctx'], 'w_ada': out['w_ada'], 'b_ada': out['b_ada'], 'w_in': out['w_in'], 'ret_decay_logit': out['ret_decay_logit'], 'dn_conv_w': out['dn_conv_w'], 'dn_a_log': out['dn_a_log'], 'dn_dt_bias': out['dn_dt_bias'], 'dn_norm_w': out['dn_norm_w'], 'att_qn_w': out['att_qn_w'], 'att_kn_w': out['att_kn_w'], 'w_o': out['w_o'], 'ln1_w': out['ln1_w'], 'ln1_b': out['ln1_b'], 'w_ffn_in': out['w_ffn_in'], 'w_ffn_out': out['w_ffn_out'], 'ln2_w': out['ln2_w'], 'ln2_b': out['ln2_b'], 'loss_target': out['loss_target'], 'm_c_ctx': out['m_c_ctx'], 'm_w_ada': out['m_w_ada'], 'm_b_ada': out['m_b_ada'], 'm_w_in': out['m_w_in'], 'm_ret_decay_logit': out['m_ret_decay_logit'], 'm_dn_conv_w': out['m_dn_conv_w'], 'm_dn_a_log': out['m_dn_a_log'], 'm_dn_dt_bias': out['m_dn_dt_bias'], 'm_dn_norm_w': out['m_dn_norm_w'], 'm_att_qn_w': out['m_att_qn_w'], 'm_att_kn_w': out['m_att_kn_w'], 'm_w_o': out['m_w_o'], 'm_ln1_w': out['m_ln1_w'], 'm_ln1_b': out['m_ln1_b'], 'm_w_ffn_in': out['m_w_ffn_in'], 'm_w_ffn_out': out['m_w_ffn_out'], 'm_ln2_w': out['m_ln2_w'], 'm_ln2_b': out['m_ln2_b'], 'v_c_ctx': out['v_c_ctx'], 'v_w_ada': out['v_w_ada'], 'v_b_ada': out['v_b_ada'], 'v_w_in': out['v_w_in'], 'v_ret_decay_logit': out['v_ret_decay_logit'], 'v_dn_conv_w': out['v_dn_conv_w'], 'v_dn_a_log': out['v_dn_a_log'], 'v_dn_dt_bias': out['v_dn_dt_bias'], 'v_dn_norm_w': out['v_dn_norm_w'], 'v_att_qn_w': out['v_att_qn_w'], 'v_att_kn_w': out['v_att_kn_w'], 'v_w_o': out['v_w_o'], 'v_ln1_w': out['v_ln1_w'], 'v_ln1_b': out['v_ln1_b'], 'v_w_ffn_in': out['v_w_ffn_in'], 'v_w_ffn_out': out['v_w_ffn_out'], 'v_ln2_w': out['v_ln2_w'], 'v_ln2_b': out['v_ln2_b']}


def _loss(weights, diff, rest, loss_target):
    with _jax.named_scope("forward"):
        args = {**rest, TWIN_DIFF_INPUT: diff, **{k: w.astype(_WEIGHT_DTYPES[k]) for k, w in weights.items()}}
        y = _forward(args)
    with _jax.named_scope("loss_head"):
        err = _jnp.square(y.astype(_jnp.float32) - loss_target)
        return 0.5 * _jnp.sum(_jnp.mean(err, axis=-1)) if err.ndim else 0.5 * err


def _adamw(w, g, m, v):
    m = ADAM_B1 * m + (1.0 - ADAM_B1) * g
    v = ADAM_B2 * v + (1.0 - ADAM_B2) * _jnp.square(g)
    m_hat = m / (1.0 - ADAM_B1 ** ADAM_STEP)
    v_hat = v / (1.0 - ADAM_B2 ** ADAM_STEP)
    delta = -ADAM_LR * (m_hat / (_jnp.sqrt(v_hat) + ADAM_EPS) + ADAM_WD * w)
    return delta, m, v


def reference(x, c, ctx, c_ctx, w_ada, b_ada, w_in, ret_decay_logit, dn_conv_w, dn_a_log, dn_dt_bias, dn_norm_w, att_qn_w, att_kn_w, w_o, ln1_w, ln1_b, w_ffn_in, w_ffn_out, ln2_w, ln2_b, loss_target, m_c_ctx, m_w_ada, m_b_ada, m_w_in, m_ret_decay_logit, m_dn_conv_w, m_dn_a_log, m_dn_dt_bias, m_dn_norm_w, m_att_qn_w, m_att_kn_w, m_w_o, m_ln1_w, m_ln1_b, m_w_ffn_in, m_w_ffn_out, m_ln2_w, m_ln2_b, v_c_ctx, v_w_ada, v_b_ada, v_w_in, v_ret_decay_logit, v_dn_conv_w, v_dn_a_log, v_dn_dt_bias, v_dn_norm_w, v_att_qn_w, v_att_kn_w, v_w_o, v_ln1_w, v_ln1_b, v_w_ffn_in, v_w_ffn_out, v_ln2_w, v_ln2_b):
    given = dict(x=x, c=c, ctx=ctx, c_ctx=c_ctx, w_ada=w_ada, b_ada=b_ada, w_in=w_in, ret_decay_logit=ret_decay_logit, dn_conv_w=dn_conv_w, dn_a_log=dn_a_log, dn_dt_bias=dn_dt_bias, dn_norm_w=dn_norm_w, att_qn_w=att_qn_w, att_kn_w=att_kn_w, w_o=w_o, ln1_w=ln1_w, ln1_b=ln1_b, w_ffn_in=w_ffn_in, w_ffn_out=w_ffn_out, ln2_w=ln2_w, ln2_b=ln2_b, loss_target=loss_target, m_c_ctx=m_c_ctx, m_w_ada=m_w_ada, m_b_ada=m_b_ada, m_w_in=m_w_in, m_ret_decay_logit=m_ret_decay_logit, m_dn_conv_w=m_dn_conv_w, m_dn_a_log=m_dn_a_log, m_dn_dt_bias=m_dn_dt_bias, m_dn_norm_w=m_dn_norm_w, m_att_qn_w=m_att_qn_w, m_att_kn_w=m_att_kn_w, m_w_o=m_w_o, m_ln1_w=m_ln1_w, m_ln1_b=m_ln1_b, m_w_ffn_in=m_w_ffn_in, m_w_ffn_out=m_w_ffn_out, m_ln2_w=m_ln2_w, m_ln2_b=m_ln2_b, v_c_ctx=v_c_ctx, v_w_ada=v_w_ada, v_b_ada=v_b_ada, v_w_in=v_w_in, v_ret_decay_logit=v_ret_decay_logit, v_dn_conv_w=v_dn_conv_w, v_dn_a_log=v_dn_a_log, v_dn_dt_bias=v_dn_dt_bias, v_dn_norm_w=v_dn_norm_w, v_att_qn_w=v_att_qn_w, v_att_kn_w=v_att_kn_w, v_w_o=v_w_o, v_ln1_w=v_ln1_w, v_ln1_b=v_ln1_b, v_w_ffn_in=v_w_ffn_in, v_w_ffn_out=v_w_ffn_out, v_ln2_w=v_ln2_w, v_ln2_b=v_ln2_b)
    weights = {n: given[n] for n in TWIN_WEIGHTS}
    shared = {n: given[n] for n in SHARED_INPUTS}
    per_example = {n: given[n] for n in ['x', 'c', 'ctx']}
    grad_fn = _jax.value_and_grad(_loss, argnums=(0, 1))

    def one_microbatch(ex, loss_target):
        ex = dict(ex)
        diff = ex.pop(TWIN_DIFF_INPUT)
        return grad_fn(weights, diff, {**shared, **ex}, loss_target)

    if N_MICROBATCH == 1:
        loss, (grad_w, grad_x) = one_microbatch(per_example, given["loss_target"])
    else:
        def body(carry, xs):
            loss_sum, grad_sum = carry
            l_k, (gw_k, gx_k) = one_microbatch(xs[0], xs[1])
            with _jax.named_scope("update"):
                return (loss_sum + l_k, _jax.tree.map(_jnp.add, grad_sum, gw_k)), gx_k

        init = (_jnp.zeros((), _jnp.float32), _jax.tree.map(_jnp.zeros_like, weights))
        (loss, grad_w), grad_x = _jax.lax.scan(body, init, (per_example, given["loss_target"]))
    with _jax.named_scope("update"):
        delta_w, new_m, new_v = {}, {}, {}
        for n in TWIN_WEIGHTS:
            delta_w[n], new_m[n], new_v[n] = _adamw(weights[n], grad_w[n], given["m_" + n], given["v_" + n])
    return (loss, grad_x, *[grad_w[n] for n in TWIN_WEIGHTS], *[delta_w[n] for n in TWIN_WEIGHTS],
            *[new_m[n] for n in TWIN_WEIGHTS], *[new_v[n] for n in TWIN_WEIGHTS])
```

```python
import functools
import math

import jax
import jax.numpy as jnp
import numpy as np
from jax import lax
from jax.experimental import pallas as pl
from jax.experimental.pallas import tpu as pltpu

F32 = jnp.float32
BF16 = jnp.bfloat16
HI = lax.Precision.HIGHEST

HEAD_DIM = 128
RET_HEADS = 4
DN_HEADS = 4
ATT_HEADS = 8
ATT_KV_HEADS = 2
RET_CHUNK = 128
DN_CHUNK = 64
DN_CONV_K = 5
GRID_W = 64
ROPE_THETA = 10000.0
EPS = 1e-6
ADAM_LR = 0.001
ADAM_B1 = 0.9
ADAM_B2 = 0.999
ADAM_EPS = 1e-08
ADAM_WD = 0.01
ADAM_STEP = 10
N_DEV = 8
LANE = 128
ROW_TILE = 256
MESH = pl.DeviceIdType.MESH


def _tile(n, cap, unit=LANE):
    best = None
    for t in range(unit, min(n, cap) + 1, unit):
        if n % t == 0:
            best = t
    return n if best is None else best


def _op(fwd, bwd):
    @jax.custom_vjp
    def op(*args):
        return fwd(*args)[0]
    op.defvjp(fwd, bwd)
    return op


def _cast(x, exact):
    return x if exact else x.astype(BF16)


def _dot(a, b, dims, exact):
    return lax.dot_general(_cast(a, exact), _cast(b, exact), (dims, ((), ())),
                           precision=HI if exact else None, preferred_element_type=F32)


def mm_nn(name, a, b, exact=False, tm_cap=1024, tn_cap=1408, tk_cap=2048):
    m, k = a.shape
    slotted = b.ndim == 3
    if slotted:
        s, _, ns = b.shape
        n = s * ns
        tn = _tile(ns, tn_cap)
        per = ns // tn
    else:
        n = b.shape[1]
        tn = _tile(n, tn_cap)
    tm = _tile(m, tm_cap, 8)
    tk = _tile(k, tk_cap)
    nk = k // tk

    def body(a_ref, b_ref, o_ref):
        kk = pl.program_id(2)

        @pl.when(kk == 0)
        def _():
            o_ref[...] = jnp.zeros_like(o_ref)
        o_ref[...] += _dot(a_ref[...], b_ref[...], ((1,), (0,)), exact)

    if slotted:
        b_spec = pl.BlockSpec((None, tk, tn), lambda i, j, kk: (j // per, kk, j % per))
    else:
        b_spec = pl.BlockSpec((tk, tn), lambda i, j, kk: (kk, j))
    return pl.pallas_call(
        body, name=name, grid=(m // tm, n // tn, nk),
        in_specs=[pl.BlockSpec((tm, tk), lambda i, j, kk: (i, kk)), b_spec],
        out_specs=pl.BlockSpec((tm, tn), lambda i, j, kk: (i, j)),
        out_shape=jax.ShapeDtypeStruct((m, n), F32),
        compiler_params=pltpu.CompilerParams(dimension_semantics=("parallel", "parallel", "arbitrary")),
    )(a, b)


def mm_nt(name, a, b, exact=False, tm_cap=1024, tn_cap=2048, tk_cap=1408):
    m, n = a.shape
    slotted = b.ndim == 3
    if slotted:
        s, k, ns = b.shape
        tk = _tile(ns, tk_cap)
        per = ns // tk
    else:
        k = b.shape[0]
        tk = _tile(n, tk_cap)
    tm = _tile(m, tm_cap, 8)
    tn = _tile(k, tn_cap)
    nk = n // tk

    def body(a_ref, b_ref, o_ref):
        kk = pl.program_id(2)

        @pl.when(kk == 0)
        def _():
            o_ref[...] = jnp.zeros_like(o_ref)
        o_ref[...] += _dot(a_ref[...], b_ref[...], ((1,), (1,)), exact)

    if slotted:
        b_spec = pl.BlockSpec((None, tn, tk), lambda i, j, kk: (kk // per, j, kk % per))
    else:
        b_spec = pl.BlockSpec((tn, tk), lambda i, j, kk: (j, kk))
    return pl.pallas_call(
        body, name=name, grid=(m // tm, k // tn, nk),
        in_specs=[pl.BlockSpec((tm, tk), lambda i, j, kk: (i, kk)), b_spec],
        out_specs=pl.BlockSpec((tm, tn), lambda i, j, kk: (i, j)),
        out_shape=jax.ShapeDtypeStruct((m, k), F32),
        compiler_params=pltpu.CompilerParams(dimension_semantics=("parallel", "parallel", "arbitrary")),
    )(a, b)


def mm_tn(name, a, b, slots=None, exact=False, tm_cap=1024, tn_cap=1408, tk_cap=1024):
    m, k = a.shape
    n = b.shape[1]
    if slots is not None:
        s, ns = slots
        tn = _tile(ns, tn_cap)
        per = ns // tn
        out_shape = jax.ShapeDtypeStruct((s, k, ns), F32)
    else:
        tn = _tile(n, tn_cap)
        out_shape = jax.ShapeDtypeStruct((k, n), F32)
    tm = _tile(k, tm_cap)
    tk = _tile(m, tk_cap, 8)
    nk = m // tk

    def body(a_ref, b_ref, o_ref):
        kk = pl.program_id(2)

        @pl.when(kk == 0)
        def _():
            o_ref[...] = jnp.zeros_like(o_ref)
        o_ref[...] += _dot(a_ref[...], b_ref[...], ((0,), (0,)), exact)

    if slots is not None:
        o_spec = pl.BlockSpec((None, tm, tn), lambda i, j, kk: (j // per, i, j % per))
    else:
        o_spec = pl.BlockSpec((tm, tn), lambda i, j, kk: (i, j))
    return pl.pallas_call(
        body, name=name, grid=(k // tm, n // tn, nk),
        in_specs=[pl.BlockSpec((tk, tm), lambda i, j, kk: (kk, i)), pl.BlockSpec((tk, tn), lambda i, j, kk: (kk, j))],
        out_specs=o_spec, out_shape=out_shape,
        compiler_params=pltpu.CompilerParams(dimension_semantics=("parallel", "parallel", "arbitrary")),
    )(a, b)


def linear(name, a, w, like):
    def fwd(a, w, like):
        return mm_nn(name + "_fwd", a, w), (a, w)

    def bwd(res, dy):
        a, w = res
        da = mm_nt(name + "_dx", dy, w)
        slots = (w.shape[0], w.shape[2]) if w.ndim == 3 else None
        dw = mm_tn(name + "_dw", a, dy, slots=slots)
        return da, jnp.zeros_like(w), dw

    return _op(fwd, bwd)(a, w, like)


def rowwise(name, f, rows, row_modes, row_diff, vecs, vec_kinds, vec_diff, out_defs, ncol=1, nctx=0):
    nr, nv = len(rows), len(vecs)
    t = rows[0].shape[0]
    tm = min(ROW_TILE, t)
    nrow = t // tm

    def row_spec(a, mode):
        w = a.shape[1]
        if mode == 'j':
            return pl.BlockSpec((tm, w // ncol), lambda i, j: (i, j))
        return pl.BlockSpec((tm, w), lambda i, j: (i, 0))

    def vec_spec(a, kind):
        nd = a.ndim
        if kind == 'shared':
            return pl.BlockSpec(a.shape, lambda i, j: (0,) * nd)
        return pl.BlockSpec((None,) + a.shape[1:], lambda i, j: ((i >= nctx).astype(jnp.int32),) + (0,) * (nd - 1))

    in_specs = [row_spec(a, m) for a, m in zip(rows, row_modes)] + [vec_spec(a, k) for a, k in zip(vecs, vec_kinds)]
    out_specs = [pl.BlockSpec((tm, w), lambda i, j: (i, j)) for w, _ in out_defs]
    out_shape = [jax.ShapeDtypeStruct((t, ncol * w), dt) for w, dt in out_defs]
    params = pltpu.CompilerParams(dimension_semantics=("arbitrary", "arbitrary"))

    def fwd_call(*args):
        def body(*refs):
            vals = [r[...] for r in refs[:nr + nv]]
            outs = f(*vals)
            for o_ref, o in zip(refs[nr + nv:], outs):
                o_ref[...] = o.astype(o_ref.dtype)
        return pl.pallas_call(body, name=name + "_fwd", grid=(nrow, ncol), in_specs=in_specs, out_specs=out_specs,
                              out_shape=out_shape, compiler_params=params)(*args)

    diff_idx = [i for i in range(nr) if row_diff[i]] + [nr + i for i in range(nv) if vec_diff[i]]
    d_rows = [i for i in range(nr) if row_diff[i]]
    d_vecs = [i for i in range(nv) if vec_diff[i]]

    def bwd_call(args, cts):
        n_in = nr + nv + len(out_defs)

        def body(*refs):
            i, j = pl.program_id(0), pl.program_id(1)
            vals = [r[...] for r in refs[:nr + nv]]
            ct = tuple(r[...] for r in refs[nr + nv:n_in])

            def g(*dvals):
                full = list(vals)
                for idx, v in zip(diff_idx, dvals):
                    full[idx] = v
                return tuple(f(*full))

            outs, vjp = jax.vjp(g, *[vals[idx] for idx in diff_idx])
            grads = vjp(tuple(c.astype(o.dtype) for c, o in zip(ct, outs)))
            out_refs = refs[n_in:]
            for p, _ in enumerate(d_rows):
                out_refs[p][...] = grads[p].astype(out_refs[p].dtype)
            for p, vi in enumerate(d_vecs):
                ref = out_refs[len(d_rows) + p]
                if vec_kinds[vi] == 'shared':
                    first = jnp.logical_and(i == 0, j == 0)
                else:
                    first = jnp.logical_and(jnp.logical_or(i == 0, i == nctx), j == 0)

                @pl.when(first)
                def _():
                    ref[...] = jnp.zeros_like(ref)
                ref[...] += grads[len(d_rows) + p].astype(F32)

        ct_specs = [pl.BlockSpec((tm, w), lambda i, j: (i, j)) for w, _ in out_defs]
        g_specs = [row_spec(rows[i], row_modes[i]) for i in d_rows] + [vec_spec(vecs[i], vec_kinds[i]) for i in d_vecs]
        g_shape = [jax.ShapeDtypeStruct(rows[i].shape, rows[i].dtype) for i in d_rows] + \
                  [jax.ShapeDtypeStruct(vecs[i].shape, F32) for i in d_vecs]
        return pl.pallas_call(body, name=name + "_bwd", grid=(nrow, ncol), in_specs=in_specs + ct_specs, out_specs=g_specs,
                              out_shape=g_shape, compiler_params=params)(*args, *cts)

    def fwd(*args):
        return tuple(fwd_call(*args)), args

    def bwd(args, cts):
        grads = bwd_call(args, cts)
        out = [None] * (nr + nv)
        for p, idx in enumerate(diff_idx):
            out[idx] = grads[p]
        for idx in range(nr + nv):
            if out[idx] is None:
                out[idx] = jnp.zeros_like(args[idx])
        return tuple(out)

    for i in d_rows:
        assert row_modes[i] == 'j' or ncol == 1
    return _op(fwd, bwd)(*rows, *vecs)


def _silu(x):
    return x * jax.nn.sigmoid(x)


def _roll_half(x):
    return pltpu.roll(x, HEAD_DIM // 2, axis=1)


@jax.custom_vjp
def _rope(x, cosf, sins):
    return x * cosf + _roll_half(x) * sins


def _rope_fwd(x, cosf, sins):
    return _rope(x, cosf, sins), (cosf, sins)


def _rope_bwd(res, dy):
    cosf, sins = res
    return dy * cosf + _roll_half(dy * sins), jnp.zeros_like(cosf), jnp.zeros_like(sins)


_rope.defvjp(_rope_fwd, _rope_bwd)


def _f_mod(x, shift, scale):
    return (x * (1.0 + scale) + shift,)


def _layer_norm(z, w, b):
    mu = jnp.mean(z, -1, keepdims=True)
    zc = z - mu
    var = jnp.mean(zc * zc, -1, keepdims=True)
    return zc * lax.rsqrt(var + EPS) * w + b


def _f_norm_mod(alpha, x, t, gate, w, b, shift, scale):
    xn = _layer_norm(alpha * x + gate * t, w, b)
    return xn, xn * (1.0 + scale) + shift


def _f_norm(alpha, x, t, gate, w, b):
    return (_layer_norm(alpha * x + gate * t, w, b),)


def _f_ret_prep(q, k, cosf, sins):
    return _rope(q, cosf, sins), _rope(k, cosf, sins) * HEAD_DIM ** -0.5


def _rms(x):
    return x * lax.rsqrt(jnp.mean(x * x, -1, keepdims=True) + EPS)


def _f_gated_out(of, ob, gate):
    return (_rms(of + ob) * _silu(gate),)


def _f_gated_out_w(of, ob, gate, w):
    return (_rms(of + ob) * w * _silu(gate),)


def _l2n(x):
    return x * lax.rsqrt(jnp.sum(x * x, -1, keepdims=True) + EPS)


def _f_dn_prep(cq, ck, cv):
    return _l2n(_silu(cq)) * HEAD_DIM ** -0.5, _l2n(_silu(ck)), _silu(cv)


def _f_qk_norm(x, cosf, sins, w):
    return (_rope(_rms(x) * w, cosf, sins),)


def _f_gates(ab, alog, dtb):
    tm = ab.shape[0]
    lane = lax.broadcasted_iota(jnp.int32, (1, LANE), 1)
    g = -jnp.exp(alog) * jax.nn.softplus(ab + dtb)
    beta = jax.nn.sigmoid(ab)
    r = lax.broadcasted_iota(jnp.int32, (tm, tm), 0)
    c = lax.broadcasted_iota(jnp.int32, (tm, tm), 1)
    same = (r // DN_CHUNK) == (c // DN_CHUNK)
    lower = jnp.where(jnp.logical_and(same, c <= r), 1.0, 0.0).astype(F32)
    upper = jnp.where(jnp.logical_and(same, c >= r), 1.0, 0.0).astype(F32)
    gl = jnp.dot(lower, g, precision=HI, preferred_element_type=F32)
    gu = jnp.dot(upper, g, precision=HI, preferred_element_type=F32)
    out = jnp.where(lane < DN_HEADS, gl, jnp.where(lane < 2 * DN_HEADS, gu, jnp.where(lane < 4 * DN_HEADS, beta, 0.0)))
    return (out,)


def swiglu(name, u, tn_cap=1408):
    t, f2 = u.shape
    ff = f2 // 2
    tn = _tile(ff, tn_cap)
    nc = ff // tn
    tm = min(ROW_TILE, t)

    def fwd_call(u):
        def body(g_ref, u_ref, o_ref):
            o_ref[...] = _silu(g_ref[...]) * u_ref[...]
        return pl.pallas_call(
            body, name=name + "_fwd", grid=(t // tm, nc),
            in_specs=[pl.BlockSpec((tm, tn), lambda i, j: (i, j)), pl.BlockSpec((tm, tn), lambda i, j: (i, j + nc))],
            out_specs=pl.BlockSpec((tm, tn), lambda i, j: (i, j)), out_shape=jax.ShapeDtypeStruct((t, ff), F32),
            compiler_params=pltpu.CompilerParams(dimension_semantics=("parallel", "parallel")))(u, u)

    def bwd_call(u, da):
        def body(own_ref, other_ref, da_ref, o_ref):
            j = pl.program_id(1)
            own, other, d = own_ref[...], other_ref[...], da_ref[...]
            sg = jax.nn.sigmoid(own)
            d_gate = d * other * (sg * (1.0 + own * (1.0 - sg)))
            d_up = d * _silu(other)
            o_ref[...] = jnp.where(j < nc, d_gate, d_up)
        return pl.pallas_call(
            body, name=name + "_bwd", grid=(t // tm, 2 * nc),
            in_specs=[pl.BlockSpec((tm, tn), lambda i, j: (i, j)), pl.BlockSpec((tm, tn), lambda i, j: (i, (j + nc) % (2 * nc))),
                      pl.BlockSpec((tm, tn), lambda i, j: (i, j % nc))],
            out_specs=pl.BlockSpec((tm, tn), lambda i, j: (i, j)), out_shape=jax.ShapeDtypeStruct((t, f2), F32),
            compiler_params=pltpu.CompilerParams(dimension_semantics=("parallel", "parallel")))(u, u, da)

    return _op(lambda u: (fwd_call(u), (u,)), lambda res, da: (bwd_call(res[0], da),))(u)


HALO = 8


def _conv_specs(t, c, tm, tc):
    nb8 = t // HALO
    per = tm // HALO
    cur = pl.BlockSpec((tm, tc), lambda j, i: (i, j))
    prev = pl.BlockSpec((HALO, tc), lambda j, i: (jnp.maximum(i * per - 1, 0), j))
    nxt = pl.BlockSpec((HALO, tc), lambda j, i: (jnp.minimum((i + 1) * per, nb8 - 1), j))
    return cur, prev, nxt


def _extended(prev_ref, cur_ref, next_ref, ext_ref, i, nrow, nctx, tm):
    has_prev = jnp.logical_and(i != 0, i != nctx)
    has_next = jnp.logical_and(i != nrow - 1, i != nctx - 1)
    ext_ref[0:HALO, :] = jnp.where(has_prev, prev_ref[...], 0.0)
    ext_ref[HALO:HALO + tm, :] = cur_ref[...]
    ext_ref[HALO + tm:, :] = jnp.where(has_next, next_ref[...], 0.0)


def _conv_call(name, x, w8, nctx, flip):
    t, c = x.shape
    tm = min(ROW_TILE, t)
    tc = _tile(c, 512)
    nrow = t // tm
    pad = DN_CONV_K // 2

    def body(cur_ref, prev_ref, next_ref, w_ref, o_ref, ext_ref):
        i = pl.program_id(1)
        _extended(prev_ref, cur_ref, next_ref, ext_ref, i, nrow, nctx, tm)
        acc = jnp.zeros((tm, tc), F32)
        for j in range(DN_CONV_K):
            wj = w_ref[(DN_CONV_K - 1 - j) if flip else j, :][None, :]
            acc = acc + wj * ext_ref[HALO - pad + j:HALO - pad + j + tm, :]
        o_ref[...] = acc

    cur, prev, nxt = _conv_specs(t, c, tm, tc)
    return pl.pallas_call(
        body, name=name, grid=(c // tc, nrow),
        in_specs=[cur, prev, nxt, pl.BlockSpec((8, tc), lambda j, i: (0, j))],
        out_specs=pl.BlockSpec((tm, tc), lambda j, i: (i, j)), out_shape=jax.ShapeDtypeStruct((t, c), F32),
        scratch_shapes=[pltpu.VMEM((tm + 2 * HALO, tc), F32)],
        compiler_params=pltpu.CompilerParams(dimension_semantics=("arbitrary", "arbitrary")))(x, x, x, w8)


def _conv_dw_call(name, x, dy, nctx):
    t, c = x.shape
    tm = min(ROW_TILE, t)
    tc = _tile(c, 512)
    nrow = t // tm
    pad = DN_CONV_K // 2

    def body(cur_ref, prev_ref, next_ref, dy_ref, o_ref, ext_ref):
        i = pl.program_id(1)
        _extended(prev_ref, cur_ref, next_ref, ext_ref, i, nrow, nctx, tm)

        @pl.when(i == 0)
        def _():
            o_ref[...] = jnp.zeros_like(o_ref)
        dy = dy_ref[...]
        rows = [jnp.sum(dy * ext_ref[HALO - pad + j:HALO - pad + j + tm, :], axis=0, keepdims=True) for j in range(DN_CONV_K)]
        rows += [jnp.zeros((1, tc), F32)] * (8 - DN_CONV_K)
        o_ref[...] += jnp.concatenate(rows, axis=0)

    cur, prev, nxt = _conv_specs(t, c, tm, tc)
    return pl.pallas_call(
        body, name=name, grid=(c // tc, nrow),
        in_specs=[cur, prev, nxt, pl.BlockSpec((tm, tc), lambda j, i: (i, j))],
        out_specs=pl.BlockSpec((8, tc), lambda j, i: (0, j)), out_shape=jax.ShapeDtypeStruct((8, c), F32),
        scratch_shapes=[pltpu.VMEM((tm + 2 * HALO, tc), F32)],
        compiler_params=pltpu.CompilerParams(dimension_semantics=("arbitrary", "arbitrary")))(x, x, x, dy)


def short_conv(name, x, w8, nctx):
    def fwd(x, w8):
        return _conv_call(name + "_fwd", x, w8, nctx, False), (x, w8)

    def bwd(res, dy):
        x, w8 = res
        return _conv_call(name + "_dx", dy, w8, nctx, True), _conv_dw_call(name + "_dw", x, dy, nctx)

    return _op(fwd, bwd)(x, w8)


def _visit_pos(dirn, n, ncc, nc):
    bwd = jnp.where(n < ncc, ncc - 1 - n, ncc + nc - 1 - n)
    return jnp.where(dirn == 0, n, bwd)


def chunk_scan(name, f, chunk, heads, ncc, q, k, v, gargs, pargs):
    t = q.shape[0]
    nc = t // chunk
    g_all = 2 * heads
    d = HEAD_DIM
    ng, npar = len(gargs), len(pargs)

    def specs(rev_visit):
        def vis(n):
            return (nc - 1 - n) if rev_visit else n
        qs = pl.BlockSpec((chunk, d), lambda g, n: (_visit_pos(g // heads, vis(n), ncc, nc), g % heads))
        gs = [pl.BlockSpec((None, None) + a.shape[2:], lambda g, n: (g, _visit_pos(g // heads, vis(n), ncc, nc), 0, 0)) for a in gargs]
        ps = [pl.BlockSpec((None,) + a.shape[1:], lambda g, n: (g, 0, 0)) for a in pargs]
        os_ = pl.BlockSpec((None, chunk, d), lambda g, n: (g // heads, _visit_pos(g // heads, vis(n), ncc, nc), g % heads))
        ss = pl.BlockSpec((None, None, d, d), lambda g, n: (g, vis(n), 0, 0))
        return qs, gs, ps, os_, ss

    params = pltpu.CompilerParams(dimension_semantics=("arbitrary", "arbitrary"))

    def fwd_call(q, k, v, *rest):
        qs, gs, ps, os_, ss = specs(False)

        def body(*refs):
            q_ref, k_ref, v_ref = refs[:3]
            rest_refs = refs[3:3 + ng + npar]
            o_ref, sp_ref, s_ref = refs[3 + ng + npar:]
            g, n = pl.program_id(0), pl.program_id(1)

            @pl.when(n == 0)
            def _():
                s_ref[...] = jnp.zeros_like(s_ref)
            s_prev = s_ref[...]
            sp_ref[...] = s_prev
            o, s_new = f(g // heads, q_ref[...], k_ref[...], v_ref[...], *[r[...] for r in rest_refs], s_prev)
            o_ref[...] = o
            s_ref[...] = s_new

        return pl.pallas_call(
            body, name=name + "_fwd", grid=(g_all, nc), in_specs=[qs, qs, qs] + gs + ps, out_specs=[os_, ss],
            out_shape=[jax.ShapeDtypeStruct((2, t, heads * d), F32), jax.ShapeDtypeStruct((g_all, nc, d, d), F32)],
            scratch_shapes=[pltpu.VMEM((d, d), F32)], compiler_params=params)(q, k, v, *rest)

    def bwd_call(q, k, v, rest, s_prev, do):
        qs, gs, ps, os_, ss = specs(True)

        def body(*refs):
            q_ref, k_ref, v_ref = refs[:3]
            rest_refs = refs[3:3 + ng + npar]
            sp_ref, do_ref = refs[3 + ng + npar:5 + ng + npar]
            outs = refs[5 + ng + npar:]
            dq_ref, dk_ref, dv_ref = outs[:3]
            dg_refs = outs[3:3 + ng]
            dp_refs = outs[3 + ng:3 + ng + npar]
            ds_ref = outs[3 + ng + npar]
            g, n = pl.program_id(0), pl.program_id(1)

            @pl.when(n == 0)
            def _():
                ds_ref[...] = jnp.zeros_like(ds_ref)
                for r in dp_refs:
                    r[...] = jnp.zeros_like(r)
            rev = g // heads
            vals = [q_ref[...], k_ref[...], v_ref[...]] + [r[...] for r in rest_refs] + [sp_ref[...]]
            _, vjp = jax.vjp(lambda *a: f(rev, *a), *vals)
            grads = vjp((do_ref[...], ds_ref[...]))
            dq_ref[...] = grads[0]
            dk_ref[...] = grads[1]
            dv_ref[...] = grads[2]
            for r, gr in zip(dg_refs, grads[3:3 + ng]):
                r[...] = gr
            for r, gr in zip(dp_refs, grads[3 + ng:3 + ng + npar]):
                r[...] += gr
            ds_ref[...] = grads[3 + ng + npar]

        dshape = jax.ShapeDtypeStruct((2, t, heads * d), F32)
        return pl.pallas_call(
            body, name=name + "_bwd", grid=(g_all, nc), in_specs=[qs, qs, qs] + gs + ps + [ss, os_],
            out_specs=[os_, os_, os_] + gs + ps,
            out_shape=[dshape, dshape, dshape] + [jax.ShapeDtypeStruct(a.shape, F32) for a in gargs] +
                      [jax.ShapeDtypeStruct(a.shape, F32) for a in pargs],
            scratch_shapes=[pltpu.VMEM((d, d), F32)], compiler_params=params)(q, k, v, *rest, s_prev, do)

    def fwd(q, k, v, *rest):
        o, s_prev = fwd_call(q, k, v, *rest)
        return o, (q, k, v, rest, s_prev)

    def bwd(res, do):
        q, k, v, rest, s_prev = res
        grads = bwd_call(q, k, v, rest, s_prev, do)
        dq, dk, dv = (gr[0] + gr[1] for gr in grads[:3])
        return (dq, dk, dv) + tuple(grads[3:])

    return _op(fwd, bwd)(q, k, v, *gargs, *pargs)


def _dir_sign(rev):
    r = rev.astype(jnp.int32)
    return 1 - 2 * r, r


def _f_retention(rev, q, k, v, logit, s):
    c = RET_CHUNK
    sgn, r = _dir_sign(rev)
    lg = jax.nn.log_sigmoid(logit)
    i = lax.broadcasted_iota(jnp.int32, (c, c), 0)
    j = lax.broadcasted_iota(jnp.int32, (c, c), 1)
    rel = ((i - j) * sgn).astype(F32)
    decay = jnp.where(rel >= 0, jnp.exp(jnp.maximum(rel, 0.0) * lg), 0.0)
    intra = lax.dot_general(q, k, (((1,), (1,)), ((), ())), precision=HI, preferred_element_type=F32) * decay
    o_intra = jnp.dot(intra, v, precision=HI, preferred_element_type=F32)
    pos = lax.broadcasted_iota(jnp.int32, (c, 1), 0)
    p = (pos * sgn + r * (c - 1)).astype(F32)
    q_decay = jnp.exp((p + 1.0) * lg)
    k_decay = jnp.exp((c - 1.0 - p) * lg)
    o_inter = jnp.dot(q * q_decay, s, precision=HI, preferred_element_type=F32)
    s_new = s * jnp.exp(c * lg) + lax.dot_general(k * k_decay, v, (((0,), (0,)), ((), ())), precision=HI, preferred_element_type=F32)
    return o_intra + o_inter, s_new


def _f_delta(rev, q, k, v, gc, gr, bc, s):
    c = DN_CHUNK
    sgn, r = _dir_sign(rev)
    i = lax.broadcasted_iota(jnp.int32, (c, c), 0)
    j = lax.broadcasted_iota(jnp.int32, (c, c), 1)
    rel = (i - j) * sgn
    tri = rel >= 0
    strict = rel > 0
    dot = functools.partial(jnp.dot, precision=HI, preferred_element_type=F32)
    dot_t = lambda a, b: lax.dot_general(a, b, (((1,), (1,)), ((), ())), precision=HI, preferred_element_type=F32)
    decay = jnp.where(tri, jnp.exp(jnp.where(tri, gc - gr, 0.0)), 0.0)
    kb = k * bc
    vb = v * bc
    a = jnp.where(strict, dot_t(kb, k) * decay, 0.0)
    eye = jnp.where(i == j, 1.0, 0.0).astype(F32)
    pw = -a
    tinv = eye + pw
    for _ in range(int(math.log2(c)) - 1):
        pw = dot(pw, pw)
        tinv = tinv + dot(tinv, pw)
    eg = jnp.exp(gc)
    w_val = dot(tinv, vb)
    k_cum = dot(tinv, kb * eg)
    qk = dot_t(q, k) * decay
    q_g = q * eg
    pos = lax.broadcasted_iota(jnp.int32, (c, 1), 0)
    g_last = jnp.sum(jnp.where(pos == (1 - r) * (c - 1), gc, 0.0), axis=0, keepdims=True)
    k_g = k * jnp.exp(g_last - gc)
    v_new = w_val - dot(k_cum, s)
    o = dot(q_g, s) + dot(qk, v_new)
    s_new = s * jnp.exp(g_last) + lax.dot_general(k_g, v_new, (((0,), (0,)), ((), ())), precision=HI, preferred_element_type=F32)
    return o, s_new


def _att_tiles(tq, tk):
    return _tile(tq, 512, 8), _tile(tk, 768, 8)


def _att_fwd_call(name, q, k, v):
    tq, tk = q.shape[0], k.shape[0]
    h, kvh = q.shape[1] // HEAD_DIM, k.shape[1] // HEAD_DIM
    grp = h // kvh
    bq, bk = _att_tiles(tq, tk)
    nk = tk // bk
    scale = HEAD_DIM ** -0.5

    def body(q_ref, k_ref, v_ref, o_ref, lse_ref, m_sc, l_sc, acc_sc):
        kj = pl.program_id(2)

        @pl.when(kj == 0)
        def _():
            m_sc[...] = jnp.full_like(m_sc, -jnp.inf)
            l_sc[...] = jnp.zeros_like(l_sc)
            acc_sc[...] = jnp.zeros_like(acc_sc)
        s = _dot(q_ref[...], k_ref[...], ((1,), (1,)), False) * scale
        m_prev = m_sc[...]
        m_new = jnp.maximum(m_prev, jnp.max(s, axis=1, keepdims=True))
        p = jnp.exp(s - m_new)
        alpha = jnp.exp(m_prev - m_new)
        l_sc[...] = alpha * l_sc[...] + jnp.sum(p, axis=1, keepdims=True)
        acc_sc[...] = alpha * acc_sc[...] + _dot(p, v_ref[...], ((1,), (0,)), False)
        m_sc[...] = m_new

        @pl.when(kj == nk - 1)
        def _():
            o_ref[...] = acc_sc[...] / l_sc[...]
            lse_ref[...] = m_sc[...] + jnp.log(l_sc[...])

    return pl.pallas_call(
        body, name=name, grid=(h, tq // bq, nk),
        in_specs=[pl.BlockSpec((bq, HEAD_DIM), lambda hh, i, j: (i, hh)),
                  pl.BlockSpec((bk, HEAD_DIM), lambda hh, i, j: (j, hh // grp)),
                  pl.BlockSpec((bk, HEAD_DIM), lambda hh, i, j: (j, hh // grp))],
        out_specs=[pl.BlockSpec((bq, HEAD_DIM), lambda hh, i, j: (i, hh)),
                   pl.BlockSpec((None, bq, 1), lambda hh, i, j: (hh, i, 0))],
        out_shape=[jax.ShapeDtypeStruct((tq, h * HEAD_DIM), F32), jax.ShapeDtypeStruct((h, tq, 1), F32)],
        scratch_shapes=[pltpu.VMEM((bq, 1), F32), pltpu.VMEM((bq, 1), F32), pltpu.VMEM((bq, HEAD_DIM), F32)],
        compiler_params=pltpu.CompilerParams(dimension_semantics=("parallel", "parallel", "arbitrary")))(q, k, v)


def _att_dq_call(name, q, k, v, o, lse, do):
    tq, tk = q.shape[0], k.shape[0]
    h, kvh = q.shape[1] // HEAD_DIM, k.shape[1] // HEAD_DIM
    grp = h // kvh
    bq, bk = _att_tiles(tq, tk)
    nk = tk // bk
    scale = HEAD_DIM ** -0.5

    def body(q_ref, k_ref, v_ref, o_ref, lse_ref, do_ref, dq_ref, delta_sc):
        kj = pl.program_id(2)

        @pl.when(kj == 0)
        def _():
            dq_ref[...] = jnp.zeros_like(dq_ref)
            delta_sc[...] = jnp.sum(do_ref[...] * o_ref[...], axis=1, keepdims=True)
        s = _dot(q_ref[...], k_ref[...], ((1,), (1,)), False) * scale
        p = jnp.exp(s - lse_ref[...])
        dp = _dot(do_ref[...], v_ref[...], ((1,), (1,)), False)
        ds = p * (dp - delta_sc[...])
        dq_ref[...] += _dot(ds, k_ref[...], ((1,), (0,)), False) * scale

    qspec = pl.BlockSpec((bq, HEAD_DIM), lambda hh, i, j: (i, hh))
    kspec = pl.BlockSpec((bk, HEAD_DIM), lambda hh, i, j: (j, hh // grp))
    return pl.pallas_call(
        body, name=name, grid=(h, tq // bq, nk),
        in_specs=[qspec, kspec, kspec, qspec, pl.BlockSpec((None, bq, 1), lambda hh, i, j: (hh, i, 0)), qspec],
        out_specs=qspec, out_shape=jax.ShapeDtypeStruct(q.shape, F32),
        scratch_shapes=[pltpu.VMEM((bq, 1), F32)],
        compiler_params=pltpu.CompilerParams(dimension_semantics=("parallel", "parallel", "arbitrary")))(q, k, v, o, lse, do)


def _att_dkv_call(name, q, k, v, o, lse, do):
    tq, tk = q.shape[0], k.shape[0]
    h, kvh = q.shape[1] // HEAD_DIM, k.shape[1] // HEAD_DIM
    grp = h // kvh
    bq, bk = _att_tiles(tq, tk)
    nq = tq // bq
    nr = grp * nq
    scale = HEAD_DIM ** -0.5

    def body(q_ref, k_ref, v_ref, o_ref, lse_ref, do_ref, dk_ref, dv_ref):
        r = pl.program_id(2)

        @pl.when(r == 0)
        def _():
            dk_ref[...] = jnp.zeros_like(dk_ref)
            dv_ref[...] = jnp.zeros_like(dv_ref)
        s = _dot(q_ref[...], k_ref[...], ((1,), (1,)), False) * scale
        p = jnp.exp(s - lse_ref[...])
        dv_ref[...] += _dot(p, do_ref[...], ((0,), (0,)), False)
        dp = _dot(do_ref[...], v_ref[...], ((1,), (1,)), False)
        delta = jnp.sum(do_ref[...] * o_ref[...], axis=1, keepdims=True)
        ds = p * (dp - delta)
        dk_ref[...] += _dot(ds, q_ref[...], ((0,), (0,)), False) * scale

    qspec = pl.BlockSpec((bq, HEAD_DIM), lambda kh, j, r: (r % nq, kh * grp + r // nq))
    kspec = pl.BlockSpec((bk, HEAD_DIM), lambda kh, j, r: (j, kh))
    return pl.pallas_call(
        body, name=name, grid=(kvh, tk // bk, nr),
        in_specs=[qspec, kspec, kspec, qspec, pl.BlockSpec((None, bq, 1), lambda kh, j, r: (kh * grp + r // nq, r % nq, 0)), qspec],
        out_specs=[kspec, kspec], out_shape=[jax.ShapeDtypeStruct(k.shape, F32), jax.ShapeDtypeStruct(v.shape, F32)],
        compiler_params=pltpu.CompilerParams(dimension_semantics=("parallel", "parallel", "arbitrary")))(q, k, v, o, lse, do)


def attention(name, q, k, v):
    def fwd(q, k, v):
        o, lse = _att_fwd_call(name + "_fwd", q, k, v)
        return o, (q, k, v, o, lse)

    def bwd(res, do):
        q, k, v, o, lse = res
        dq = _att_dq_call(name + "_dq", q, k, v, o, lse, do)
        dk, dv = _att_dkv_call(name + "_dkv", q, k, v, o, lse, do)
        return dq, dk, dv

    return _op(fwd, bwd)(q, k, v)


def loss_head(y, target):
    t, d = y.shape
    tm = min(ROW_TILE, t)

    def fwd(y, target):
        def body(y_ref, t_ref, l_ref, dy_ref):
            @pl.when(pl.program_id(0) == 0)
            def _():
                l_ref[...] = jnp.zeros_like(l_ref)
            err = y_ref[...] - t_ref[...]
            row = jnp.sum(err * err, axis=1, keepdims=True) * (1.0 / d)
            l_ref[...] += 0.5 * jnp.sum(row, axis=0, keepdims=True)
            dy_ref[...] = err * (1.0 / d)
        loss, dy = pl.pallas_call(
            body, name="loss_head", grid=(t // tm,),
            in_specs=[pl.BlockSpec((tm, d), lambda i: (i, 0)), pl.BlockSpec((tm, d), lambda i: (i, 0))],
            out_specs=[pl.BlockSpec((1, 1), lambda i: (0, 0)), pl.BlockSpec((tm, d), lambda i: (i, 0))],
            out_shape=[jax.ShapeDtypeStruct((1, 1), F32), jax.ShapeDtypeStruct((t, d), F32)],
            compiler_params=pltpu.CompilerParams(dimension_semantics=("arbitrary",)))(y, target)
        return loss[0, 0], (dy,)

    def bwd(res, g):
        return res[0] * g, jnp.zeros_like(res[0])

    return _op(fwd, bwd)(y, target)


def _my_index():
    return 4 * lax.axis_index("x") + 2 * lax.axis_index("y") + lax.axis_index("c")


def exchange(name, arrays, modes):
    n = len(arrays)
    out_shape = [jax.ShapeDtypeStruct(((N_DEV,) + a.shape) if m == 'gather' else a.shape, a.dtype) for a, m in zip(arrays, modes)]

    def body(*refs):
        ins, outs = refs[:n], refs[n:2 * n]
        send_sems, recv_sems, local_sems = refs[2 * n:]
        x, y, c = lax.axis_index("x"), lax.axis_index("y"), lax.axis_index("c")
        me = 4 * x + 2 * y + c
        copies = []
        for a in range(n):
            gather = modes[a] == 'gather'
            loc = pltpu.make_async_copy(ins[a] if gather else ins[a].at[me], outs[a].at[me], local_sems.at[a])
            loc.start()
            copies.append(loc)
            for dist in range(1, N_DEV):
                px = (1 - x) if dist & 4 else x
                py = (1 - y) if dist & 2 else y
                pc = (1 - c) if dist & 1 else c
                peer = 4 * px + 2 * py + pc
                cp = pltpu.make_async_remote_copy(
                    src_ref=ins[a] if gather else ins[a].at[peer], dst_ref=outs[a].at[me],
                    send_sem=send_sems.at[a * (N_DEV - 1) + dist - 1], recv_sem=recv_sems.at[a * (N_DEV - 1) + dist - 1],
                    device_id=(px, py, pc), device_id_type=MESH)
                cp.start()
                copies.append(cp)
        for cp in copies:
            cp.wait()

    hbm = pl.BlockSpec(memory_space=pl.ANY)
    return pl.pallas_call(
        body, name=name, in_specs=[hbm] * n, out_specs=[hbm] * n, out_shape=out_shape,
        scratch_shapes=[pltpu.SemaphoreType.DMA((n * (N_DEV - 1),)), pltpu.SemaphoreType.DMA((n * (N_DEV - 1),)),
                        pltpu.SemaphoreType.DMA((n,))],
        compiler_params=pltpu.CompilerParams(has_side_effects=True))(*arrays)


def _adamw_math(w, g, m, v):
    m = ADAM_B1 * m + (1.0 - ADAM_B1) * g
    v = ADAM_B2 * v + (1.0 - ADAM_B2) * (g * g)
    m_hat = m / (1.0 - ADAM_B1 ** ADAM_STEP)
    v_hat = v / (1.0 - ADAM_B2 ** ADAM_STEP)
    delta = -ADAM_LR * (m_hat / (jnp.sqrt(v_hat) + ADAM_EPS) + ADAM_WD * w)
    return delta, m, v


def adamw(name, parts, w, m, v, scale=None, row_bytes_cap=1 << 20):
    p, r, c = parts.shape
    tr = _tile(r, max(8, row_bytes_cap // (4 * c) // 8 * 8), 8)
    extra = [] if scale is None else [scale]

    def body(p_ref, *refs):
        w_ref, m_ref, v_ref, g_ref, d_ref, nm_ref, nv_ref = refs[len(extra):]
        g = p_ref[0]
        for q in range(1, p):
            g = g + p_ref[q]
        if extra:
            g = g * refs[0][...]
        delta, nm, nv = _adamw_math(w_ref[...], g, m_ref[...], v_ref[...])
        g_ref[...] = g
        d_ref[...] = delta
        nm_ref[...] = nm
        nv_ref[...] = nv

    spec = pl.BlockSpec((tr, c), lambda i: (i, 0))
    shape = jax.ShapeDtypeStruct((r, c), F32)
    return pl.pallas_call(
        body, name=name, grid=(r // tr,),
        in_specs=[pl.BlockSpec((p, tr, c), lambda i: (0, i, 0))] + [spec] * (3 + len(extra)),
        out_specs=[spec] * 4, out_shape=[shape] * 4,
        compiler_params=pltpu.CompilerParams(dimension_semantics=("parallel",)))(parts, *extra, w, m, v)


def _rope_tables(n_lat, n_ctx):
    rows = n_lat // GRID_W
    row = jnp.repeat(jnp.arange(rows, dtype=F32), GRID_W)
    col = jnp.tile(jnp.arange(GRID_W, dtype=F32), rows)
    n_freq = HEAD_DIM // 4
    inv = ROPE_THETA ** (-jnp.arange(n_freq, dtype=F32) / n_freq)
    ang = jnp.concatenate([row[:, None] * inv, col[:, None] * inv], -1)
    cos, sin = jnp.cos(ang), jnp.sin(ang)
    cosf = jnp.concatenate([jnp.ones((n_ctx, HEAD_DIM), F32), jnp.concatenate([cos, cos], -1)], 0)
    sins = jnp.concatenate([jnp.zeros((n_ctx, HEAD_DIM), F32), jnp.concatenate([-sin, sin], -1)], 0)
    return cosf, sins


def _widths():
    rw, dw, aw, kw = RET_HEADS * HEAD_DIM, DN_HEADS * HEAD_DIM, ATT_HEADS * HEAD_DIM, ATT_KV_HEADS * HEAD_DIM
    return rw, dw, aw, kw


def _pad_in_proj(full):
    rw, dw, aw, kw = _widths()
    n_ab = 4 * DN_HEADS
    a0 = 4 * rw + 4 * dw
    main = jnp.concatenate([full[:, :a0], full[:, a0 + n_ab:]], 1)
    ab = jnp.concatenate([full[:, a0:a0 + n_ab], jnp.zeros((full.shape[0], LANE - n_ab), full.dtype)], 1)
    return jnp.concatenate([main, ab], 1)


def _unpad_in_proj(dpad):
    rw, dw, aw, kw = _widths()
    n_ab = 4 * DN_HEADS
    a0 = 4 * rw + 4 * dw
    main_w = dpad.shape[1] - LANE
    return jnp.concatenate([dpad[:, :a0], dpad[:, main_w:main_w + n_ab], dpad[:, a0:main_w]], 1)


def _chunked(a, chunk):
    t, g = a.shape
    at = a.T.reshape(g, t // chunk, chunk)
    return at[..., None], at[:, :, None, :]


def _mixer(i, h, wpad, like_in, sm, cosf, sins, n_ctx, tag):
    rw, dw, aw, kw = _widths()
    t = h.shape[0]
    nctx = n_ctx // min(ROW_TILE, t)
    p = linear(f"in_proj{tag}", h, wpad, like_in)
    o = 0
    rq, rk, rv, rg = (p[:, o + n * rw:o + (n + 1) * rw] for n in range(4))
    o += 4 * rw
    dqkv, dz = p[:, o:o + 3 * dw], p[:, o + 3 * dw:o + 4 * dw]
    o += 4 * dw
    aq, ak, av = p[:, o:o + aw], p[:, o + aw:o + aw + kw], p[:, o + aw + kw:o + aw + 2 * kw]
    ab = p[:, o + aw + 2 * kw:]

    qr, kr = rowwise(f"ret_prep{tag}", _f_ret_prep, [rq, rk, cosf, sins], ['j', 'j', 'b', 'b'], [True, True, False, False],
                     [], [], [], [(HEAD_DIM, F32)] * 2, ncol=RET_HEADS, nctx=nctx)
    logit = sm["ret_decay_logit"][i].reshape(2 * RET_HEADS, 1, 1)
    o_ret = chunk_scan(f"ret_scan{tag}", _f_retention, RET_CHUNK, RET_HEADS, n_ctx // RET_CHUNK, qr, kr, rv, [], [logit])
    y_ret, = rowwise(f"ret_out{tag}", _f_gated_out, [o_ret[0], o_ret[1], rg], ['j'] * 3, [True] * 3, [], [], [],
                     [(HEAD_DIM, F32)], ncol=RET_HEADS, nctx=nctx)

    conv = short_conv(f"dn_conv{tag}", dqkv, sm["conv_w8"][i], nctx)
    dq, dk, dv = rowwise(f"dn_prep{tag}", _f_dn_prep, [conv[:, :dw], conv[:, dw:2 * dw], conv[:, 2 * dw:]], ['j'] * 3, [True] * 3,
                         [], [], [], [(HEAD_DIM, F32)] * 3, ncol=DN_HEADS, nctx=nctx)
    pad8 = lambda a: jnp.concatenate([a.reshape(1, 2 * DN_HEADS), jnp.zeros((1, LANE - 2 * DN_HEADS), F32)], 1)
    gb, = rowwise(f"dn_gates{tag}", _f_gates, [ab], ['j'], [True], [pad8(sm["dn_a_log"][i]), pad8(sm["dn_dt_bias"][i])],
                  ['shared'] * 2, [True] * 2, [(LANE, F32)], nctx=nctx)
    g_col, g_row = _chunked(gb[:, :2 * DN_HEADS], DN_CHUNK)
    b_col, _ = _chunked(gb[:, 2 * DN_HEADS:4 * DN_HEADS], DN_CHUNK)
    o_dn = chunk_scan(f"dn_scan{tag}", _f_delta, DN_CHUNK, DN_HEADS, n_ctx // DN_CHUNK, dq, dk, dv, [g_col, g_row, b_col], [])
    y_dn, = rowwise(f"dn_out{tag}", _f_gated_out_w, [o_dn[0], o_dn[1], dz], ['j'] * 3, [True] * 3,
                    [sm["dn_norm_w"][i].reshape(1, HEAD_DIM)], ['shared'], [True], [(HEAD_DIM, F32)], ncol=DN_HEADS, nctx=nctx)

    qn, = rowwise(f"att_qn{tag}", _f_qk_norm, [aq, cosf, sins], ['j', 'b', 'b'], [True, False, False],
                  [sm["att_qn_w"][i].reshape(1, HEAD_DIM)], ['shared'], [True], [(HEAD_DIM, F32)], ncol=ATT_HEADS, nctx=nctx)
    kn, = rowwise(f"att_kn{tag}", _f_qk_norm, [ak, cosf, sins], ['j', 'b', 'b'], [True, False, False],
                  [sm["att_kn_w"][i].reshape(1, HEAD_DIM)], ['shared'], [True], [(HEAD_DIM, F32)], ncol=ATT_KV_HEADS, nctx=nctx)
    y_att_lat = attention(f"att_lat{tag}", qn[n_ctx:], kn, av)
    return y_ret, y_dn, y_att_lat, (qn, kn, av)


def _forward(x, ctx, target, mods, wts, likes, sm):
    depth = mods.shape[0]
    alpha = (2 * depth) ** 0.25
    n_ctx, d = ctx.shape
    n_lat = x.shape[0]
    cosf, sins = _rope_tables(n_lat, n_ctx)
    xs = jnp.concatenate([ctx, x], 0)
    for i in range(depth):
        last = i == depth - 1
        tag = f"_{i}"
        t = xs.shape[0]
        nctx = n_ctx // min(ROW_TILE, t)
        seg = lambda kk: mods[i, :, kk][:, None, :]
        lat = lambda kk: mods[i, 1, kk][None, :]
        h, = rowwise(f"mod1{tag}", _f_mod, [xs], ['j'], [True], [seg(0), seg(1)], ['seg'] * 2, [True] * 2, [(d, F32)], nctx=nctx)
        y_ret, y_dn, y_att_lat, (qn, kn, av) = _mixer(i, h, wts["w_in"][i], likes["w_in"][i], sm, cosf, sins, n_ctx, tag)
        vec = lambda a: a.reshape(1, d)
        ln1 = [vec(sm["ln1_w"][i]), vec(sm["ln1_b"][i])]
        ln2 = [vec(sm["ln2_w"][i]), vec(sm["ln2_b"][i])]
        if last:
            y = jnp.concatenate([y_ret[n_ctx:], y_dn[n_ctx:], y_att_lat], 1)
            xs = xs[n_ctx:]
            nctx = 0
            kinds = ['shared']
            g1, sh2, sc2, g2 = lat(2), lat(3), lat(4), lat(5)
        else:
            y_att_ctx = attention(f"att_ctx{tag}", qn[:n_ctx], kn[:n_ctx], av[:n_ctx])
            y = jnp.concatenate([y_ret, y_dn, jnp.concatenate([y_att_ctx, y_att_lat], 0)], 1)
            kinds = ['seg']
            g1, sh2, sc2, g2 = seg(2), seg(3), seg(4), seg(5)
        tt = linear(f"out_proj{tag}", y, wts["w_o"][i], likes["w_o"][i])
        x1, h2 = rowwise(f"norm1{tag}", functools.partial(_f_norm_mod, alpha), [xs, tt], ['j', 'j'], [True, True],
                         [g1] + ln1 + [sh2, sc2], kinds + ['shared'] * 2 + kinds * 2, [True] * 5, [(d, F32)] * 2, nctx=nctx)
        u = linear(f"ffn_in{tag}", h2, wts["w_ffn_in"][i], likes["w_ffn_in"][i])
        act = swiglu(f"swiglu{tag}", u)
        t2 = linear(f"ffn_out{tag}", act, wts["w_ffn_out"][i], likes["w_ffn_out"][i])
        xs, = rowwise(f"norm2{tag}", functools.partial(_f_norm, alpha), [x1, t2], ['j', 'j'], [True, True],
                      [g2] + ln2, kinds + ['shared'] * 2, [True] * 3, [(d, F32)], nctx=nctx)
    return loss_head(xs, target)


def _ada_forward(cond16, w_ada, b_loc):
    depth, d, n = w_ada.shape
    tn = _tile(n, 512)

    def body(c_ref, w_ref, b_ref, o_ref):
        o_ref[...] = jnp.dot(c_ref[...], w_ref[...], precision=HI, preferred_element_type=F32) + b_ref[...]

    return pl.pallas_call(
        body, name="ada_fwd", grid=(depth, n // tn),
        in_specs=[pl.BlockSpec((16, d), lambda l, j: (0, 0)), pl.BlockSpec((None, d, tn), lambda l, j: (l, 0, j)),
                  pl.BlockSpec((None, 1, tn), lambda l, j: (l, 0, j))],
        out_specs=pl.BlockSpec((None, 16, tn), lambda l, j: (l, 0, j)), out_shape=jax.ShapeDtypeStruct((depth, 16, n), F32),
        compiler_params=pltpu.CompilerParams(dimension_semantics=("parallel", "parallel")))(cond16, w_ada, b_loc)


def _ada_backward(cond16, w_ada, dm16):
    depth, d, n = w_ada.shape
    tn = _tile(n, 512)

    def body(c_ref, w_ref, dm_ref, gw_ref, dc_ref):
        @pl.when(jnp.logical_and(pl.program_id(0) == 0, pl.program_id(1) == 0))
        def _():
            dc_ref[...] = jnp.zeros_like(dc_ref)
        dm = dm_ref[...]
        gw_ref[...] = lax.dot_general(c_ref[...], dm, (((0,), (0,)), ((), ())), precision=HI, preferred_element_type=F32)
        dc_ref[...] += lax.dot_general(dm, w_ref[...], (((1,), (1,)), ((), ())), precision=HI, preferred_element_type=F32)

    return pl.pallas_call(
        body, name="ada_bwd", grid=(depth, n // tn),
        in_specs=[pl.BlockSpec((16, d), lambda l, j: (0, 0)), pl.BlockSpec((None, d, tn), lambda l, j: (l, 0, j)),
                  pl.BlockSpec((None, 16, tn), lambda l, j: (l, 0, j))],
        out_specs=[pl.BlockSpec((None, d, tn), lambda l, j: (l, 0, j)), pl.BlockSpec((16, d), lambda l, j: (0, 0))],
        out_shape=[jax.ShapeDtypeStruct((depth, d, n), F32), jax.ShapeDtypeStruct((16, d), F32)],
        compiler_params=pltpu.CompilerParams(dimension_semantics=("arbitrary", "arbitrary")))(cond16, w_ada, dm16)


def _silu_rows(c_all, c_ctx):
    d = c_ctx.shape[-1]
    stacked = jnp.concatenate([c_all.reshape(N_DEV, d), c_ctx.reshape(1, d), jnp.zeros((16 - N_DEV - 1, d), F32)], 0)

    def body(c_ref, o_ref, ds_ref):
        v = c_ref[...]
        row = lax.broadcasted_iota(jnp.int32, v.shape, 0)
        o_ref[...] = jnp.where(row <= N_DEV, _silu(v), 0.0)
        cc = c_ref[N_DEV:N_DEV + 1, :]
        sg = jax.nn.sigmoid(cc)
        ds_ref[...] = sg * (1.0 + cc * (1.0 - sg))

    return pl.pallas_call(body, name="cond_silu", out_shape=[jax.ShapeDtypeStruct((16, d), F32), jax.ShapeDtypeStruct((1, d), F32)])(stacked)


def _flat_pack(arrs):
    flat = jnp.concatenate([a.reshape(-1) for a in arrs])
    n = flat.shape[0]
    rows = -(-n // LANE)
    rows = -(-rows // 8) * 8
    return jnp.concatenate([flat, jnp.zeros((rows * LANE - n,), F32)]).reshape(rows, LANE)


def _flat_unpack(packed, shapes):
    flat = packed.reshape(-1)
    out, o = [], 0
    for s in shapes:
        n = int(np.prod(s))
        out.append(flat[o:o + n].reshape(s))
        o += n
    return out


SMALL = ["b_ada", "ret_decay_logit", "dn_a_log", "dn_dt_bias", "dn_norm_w", "att_qn_w", "att_kn_w", "ln1_w", "ln1_b", "ln2_w", "ln2_b"]
OUT_ORDER = ['c_ctx', 'w_ada', 'b_ada', 'w_in', 'ret_decay_logit', 'dn_conv_w', 'dn_a_log', 'dn_dt_bias', 'dn_norm_w', 'att_qn_w',
             'att_kn_w', 'w_o', 'ln1_w', 'ln1_b', 'w_ffn_in', 'w_ffn_out', 'ln2_w', 'ln2_b']


def kernel(x, c, ctx, c_ctx, w_ada, b_ada, w_in, ret_decay_logit, dn_conv_w, dn_a_log, dn_dt_bias, dn_norm_w, att_qn_w, att_kn_w, w_o, ln1_w, ln1_b, w_ffn_in, w_ffn_out, ln2_w, ln2_b, loss_target, m_c_ctx, m_w_ada, m_b_ada, m_w_in, m_ret_decay_logit, m_dn_conv_w, m_dn_a_log, m_dn_dt_bias, m_dn_norm_w, m_att_qn_w, m_att_kn_w, m_w_o, m_ln1_w, m_ln1_b, m_w_ffn_in, m_w_ffn_out, m_ln2_w, m_ln2_b, v_c_ctx, v_w_ada, v_b_ada, v_w_in, v_ret_decay_logit, v_dn_conv_w, v_dn_a_log, v_dn_dt_bias, v_dn_norm_w, v_att_qn_w, v_att_kn_w, v_w_o, v_ln1_w, v_ln1_b, v_w_ffn_in, v_w_ffn_out, v_ln2_w, v_ln2_b):
    weights = dict(c_ctx=c_ctx, w_ada=w_ada, b_ada=b_ada, w_in=w_in, ret_decay_logit=ret_decay_logit, dn_conv_w=dn_conv_w,
                   dn_a_log=dn_a_log, dn_dt_bias=dn_dt_bias, dn_norm_w=dn_norm_w, att_qn_w=att_qn_w, att_kn_w=att_kn_w, w_o=w_o,
                   ln1_w=ln1_w, ln1_b=ln1_b, w_ffn_in=w_ffn_in, w_ffn_out=w_ffn_out, ln2_w=ln2_w, ln2_b=ln2_b)
    mom1 = dict(c_ctx=m_c_ctx, w_ada=m_w_ada, b_ada=m_b_ada, w_in=m_w_in, ret_decay_logit=m_ret_decay_logit, dn_conv_w=m_dn_conv_w,
                dn_a_log=m_dn_a_log, dn_dt_bias=m_dn_dt_bias, dn_norm_w=m_dn_norm_w, att_qn_w=m_att_qn_w, att_kn_w=m_att_kn_w,
                w_o=m_w_o, ln1_w=m_ln1_w, ln1_b=m_ln1_b, w_ffn_in=m_w_ffn_in, w_ffn_out=m_w_ffn_out, ln2_w=m_ln2_w, ln2_b=m_ln2_b)
    mom2 = dict(c_ctx=v_c_ctx, w_ada=v_w_ada, b_ada=v_b_ada, w_in=v_w_in, ret_decay_logit=v_ret_decay_logit, dn_conv_w=v_dn_conv_w,
                dn_a_log=v_dn_a_log, dn_dt_bias=v_dn_dt_bias, dn_norm_w=v_dn_norm_w, att_qn_w=v_att_qn_w, att_kn_w=v_att_kn_w,
                w_o=v_w_o, ln1_w=v_ln1_w, ln1_b=v_ln1_b, w_ffn_in=v_w_ffn_in, w_ffn_out=v_w_ffn_out, ln2_w=v_ln2_w, ln2_b=v_ln2_b)
    depth, d, n_ada = w_ada.shape
    me = _my_index()
    x2, ctx2, tgt2 = x[0], ctx[0], loss_target[0]

    c_all, conv_all, win_all, wo_all, wfi_all, wfo_all = exchange(
        "gather_weights", [c, dn_conv_w, w_in.astype(BF16), w_o.astype(BF16), w_ffn_in.astype(BF16), w_ffn_out.astype(BF16)],
        ['gather'] * 6)
    proj_w = w_in.shape[2] * N_DEV
    conv_full = jnp.moveaxis(conv_all, 0, 2).reshape(depth, DN_CONV_K, -1)
    conv_w8 = jnp.concatenate([conv_full, jnp.zeros((depth, 8 - DN_CONV_K, conv_full.shape[2]), F32)], 1)
    wts = dict(
        w_in=[_pad_in_proj(jnp.moveaxis(win_all[:, i], 0, 1).reshape(d, proj_w)) for i in range(depth)],
        w_o=[wo_all[:, i].reshape(d, d) for i in range(depth)],
        w_ffn_in=[wfi_all[:, i] for i in range(depth)],
        w_ffn_out=[wfo_all[:, i].reshape(-1, d) for i in range(depth)])
    likes = {k: [jnp.zeros(a.shape, F32) for a in v] for k, v in wts.items()}

    cond16, dsilu_ctx = _silu_rows(c_all, c_ctx)
    b_loc = lax.dynamic_slice_in_dim(b_ada, me * n_ada, n_ada, axis=1)[:, None, :]
    mod_loc = _ada_forward(cond16, w_ada, b_loc)
    mod_all, = exchange("gather_mods", [mod_loc], ['gather'])
    mod_full = jnp.moveaxis(mod_all, 0, 2).reshape(depth, 16, N_DEV * n_ada)
    mods = jnp.stack([mod_full[:, N_DEV], lax.dynamic_index_in_dim(mod_full, me, 1, keepdims=False)], 1).reshape(depth, 2, 6, d)

    sm = dict(ret_decay_logit=ret_decay_logit, dn_a_log=dn_a_log, dn_dt_bias=dn_dt_bias, dn_norm_w=dn_norm_w, att_qn_w=att_qn_w,
              att_kn_w=att_kn_w, ln1_w=ln1_w, ln1_b=ln1_b, ln2_w=ln2_w, ln2_b=ln2_b, conv_w8=conv_w8)
    loss, vjp = jax.vjp(lambda xx, mm, ll, ss: _forward(xx, ctx2, tgt2, mm, wts, ll, ss), x2, mods, likes, sm)
    gx, gmods, glikes, gsm = vjp(jnp.ones((), F32))
    loss = lax.psum(loss, ("x", "y", "c"))

    gm = gmods.reshape(depth, 2, 6 * d)
    small_parts = dict(gsm)
    small_parts["b_ada"] = gm[:, 0] + gm[:, 1]
    small_pack = _flat_pack([small_parts[k] for k in SMALL])
    rw, dw, aw, kw = _widths()
    g_conv = gsm["conv_w8"][:, :DN_CONV_K].reshape(depth, DN_CONV_K, N_DEV, -1)
    parts = [
        jnp.stack([jnp.moveaxis(_unpad_in_proj(g).reshape(d, N_DEV, -1), 1, 0) for g in glikes["w_in"]], 1),
        jnp.stack([g.reshape(N_DEV, -1, d) for g in glikes["w_o"]], 1),
        jnp.stack(glikes["w_ffn_in"], 1),
        jnp.stack([g.reshape(N_DEV, -1, d) for g in glikes["w_ffn_out"]], 1),
        jnp.moveaxis(g_conv, 2, 0),
    ]
    got = exchange("exchange_grads", [gm, small_pack] + parts, ['gather', 'gather'] + ['a2a'] * 5)
    gm_all, small_all = got[0], got[1]

    ctx_sum = gm_all[0, :, 0]
    for p in range(1, N_DEV):
        ctx_sum = ctx_sum + gm_all[p, :, 0]
    dm_full = jnp.concatenate([jnp.moveaxis(gm_all[:, :, 1], 0, 1), ctx_sum[:, None], jnp.zeros((depth, 16 - N_DEV - 1, 6 * d), F32)], 1)
    dm16 = lax.dynamic_slice_in_dim(dm_full, me * n_ada, n_ada, axis=2)
    g_w_ada, dcond = _ada_backward(cond16, w_ada, dm16)
    dcond_all, = exchange("gather_dcond", [dcond[N_DEV:N_DEV + 1]], ['gather'])

    out = {}

    def update(name, parts3, shape, scale=None):
        w2 = weights[name].reshape(parts3.shape[1:])
        g, dl, nm, nv = adamw("adamw_" + name, parts3, w2, mom1[name].reshape(w2.shape), mom2[name].reshape(w2.shape), scale)
        out[name] = tuple(a.reshape(shape) for a in (g, dl, nm, nv))

    update("w_in", got[2].reshape(N_DEV, depth * d, -1), w_in.shape)
    update("w_o", got[3].reshape(N_DEV, -1, d), w_o.shape)
    update("w_ffn_in", got[4].reshape(N_DEV, depth * d, -1), w_ffn_in.shape)
    update("w_ffn_out", got[5].reshape(N_DEV, -1, d), w_ffn_out.shape)
    update("dn_conv_w", got[6].reshape(N_DEV, depth * DN_CONV_K, -1), dn_conv_w.shape)
    update("w_ada", g_w_ada.reshape(1, depth * d, n_ada), w_ada.shape)
    update("c_ctx", dcond_all.reshape(N_DEV, 1, d), c_ctx.shape, scale=dsilu_ctx)

    small_shapes = [weights[k].shape for k in SMALL]
    w_pack, m_pack, v_pack = (_flat_pack([src[k] for k in SMALL]) for src in (weights, mom1, mom2))
    packs = adamw("adamw_small", small_all, w_pack, m_pack, v_pack)
    for k, vals in zip(SMALL, zip(*[_flat_unpack(pk, small_shapes) for pk in packs])):
        out[k] = vals

    res = [loss, gx[None]]
    for slot in range(4):
        res += [out[k][slot] for k in OUT_ORDER]
    return tuple(res)
```

```python
import functools
import math

import jax
import jax.numpy as jnp
import numpy as np
from jax import lax
from jax.experimental import pallas as pl
from jax.experimental.pallas import tpu as pltpu

F32 = jnp.float32
BF16 = jnp.bfloat16
HI = lax.Precision.HIGHEST

HEAD_DIM = 128
RET_HEADS = 4
DN_HEADS = 4
ATT_HEADS = 8
ATT_KV_HEADS = 2
RET_CHUNK = 128
DN_CHUNK = 64
DN_CONV_K = 5
GRID_W = 64
ROPE_THETA = 10000.0
EPS = 1e-6
ADAM_LR = 0.001
ADAM_B1 = 0.9
ADAM_B2 = 0.999
ADAM_EPS = 1e-08
ADAM_WD = 0.01
ADAM_STEP = 10
N_DEV = 8
LANE = 128
LN2 = math.log(2.0)
LOG2E = 1.0 / LN2
ROW_TILE = 256
MESH = pl.DeviceIdType.MESH


def _tile(n, cap, unit=LANE):
    best = None
    for t in range(unit, min(n, cap) + 1, unit):
        if n % t == 0:
            best = t
    return n if best is None else best


def _op(fwd, bwd):
    @jax.custom_vjp
    def op(*args):
        return fwd(*args)[0]
    op.defvjp(fwd, bwd)
    return op


def _cast(x, exact):
    return x if exact else x.astype(BF16)


def _dot(a, b, dims, exact):
    return lax.dot_general(_cast(a, exact), _cast(b, exact), (dims, ((), ())),
                           precision=HI if exact else None, preferred_element_type=F32)


def mm_nn(name, a, b, exact=False, tm_cap=1024, tn_cap=1408, tk_cap=2048):
    m, k = a.shape
    slotted = b.ndim == 3
    if slotted:
        s, _, ns = b.shape
        n = s * ns
        tn = _tile(ns, tn_cap)
        per = ns // tn
    else:
        n = b.shape[1]
        tn = _tile(n, tn_cap)
    tm = _tile(m, tm_cap, 8)
    tk = _tile(k, tk_cap)
    nk = k // tk

    def body(a_ref, b_ref, o_ref):
        kk = pl.program_id(2)

        @pl.when(kk == 0)
        def _():
            o_ref[...] = jnp.zeros_like(o_ref)
        o_ref[...] += _dot(a_ref[...], b_ref[...], ((1,), (0,)), exact)

    if slotted:
        b_spec = pl.BlockSpec((None, tk, tn), lambda i, j, kk: (j // per, kk, j % per))
    else:
        b_spec = pl.BlockSpec((tk, tn), lambda i, j, kk: (kk, j))
    return pl.pallas_call(
        body, name=name, grid=(m // tm, n // tn, nk),
        in_specs=[pl.BlockSpec((tm, tk), lambda i, j, kk: (i, kk)), b_spec],
        out_specs=pl.BlockSpec((tm, tn), lambda i, j, kk: (i, j)),
        out_shape=jax.ShapeDtypeStruct((m, n), F32),
        compiler_params=pltpu.CompilerParams(dimension_semantics=("parallel", "parallel", "arbitrary")),
    )(a, b)


def mm_nt(name, a, b, exact=False, tm_cap=1024, tn_cap=2048, tk_cap=1408):
    m, n = a.shape
    slotted = b.ndim == 3
    if slotted:
        s, k, ns = b.shape
        tk = _tile(ns, tk_cap)
        per = ns // tk
    else:
        k = b.shape[0]
        tk = _tile(n, tk_cap)
    tm = _tile(m, tm_cap, 8)
    tn = _tile(k, tn_cap)
    nk = n // tk

    def body(a_ref, b_ref, o_ref):
        kk = pl.program_id(2)

        @pl.when(kk == 0)
        def _():
            o_ref[...] = jnp.zeros_like(o_ref)
        o_ref[...] += _dot(a_ref[...], b_ref[...], ((1,), (1,)), exact)

    if slotted:
        b_spec = pl.BlockSpec((None, tn, tk), lambda i, j, kk: (kk // per, j, kk % per))
    else:
        b_spec = pl.BlockSpec((tn, tk), lambda i, j, kk: (j, kk))
    return pl.pallas_call(
        body, name=name, grid=(m // tm, k // tn, nk),
        in_specs=[pl.BlockSpec((tm, tk), lambda i, j, kk: (i, kk)), b_spec],
        out_specs=pl.BlockSpec((tm, tn), lambda i, j, kk: (i, j)),
        out_shape=jax.ShapeDtypeStruct((m, k), F32),
        compiler_params=pltpu.CompilerParams(dimension_semantics=("parallel", "parallel", "arbitrary")),
    )(a, b)


def mm_tn(name, a, b, slots=None, exact=False, tm_cap=1024, tn_cap=1408, tk_cap=1024):
    m, k = a.shape
    n = b.shape[1]
    if slots is not None:
        s, ns = slots
        tn = _tile(ns, tn_cap)
        per = ns // tn
        out_shape = jax.ShapeDtypeStruct((s, k, ns), F32)
    else:
        tn = _tile(n, tn_cap)
        out_shape = jax.ShapeDtypeStruct((k, n), F32)
    tm = _tile(k, tm_cap)
    tk = _tile(m, tk_cap, 8)
    nk = m // tk

    def body(a_ref, b_ref, o_ref):
        kk = pl.program_id(2)

        @pl.when(kk == 0)
        def _():
            o_ref[...] = jnp.zeros_like(o_ref)
        o_ref[...] += _dot(a_ref[...], b_ref[...], ((0,), (0,)), exact)

    if slots is not None:
        o_spec = pl.BlockSpec((None, tm, tn), lambda i, j, kk: (j // per, i, j % per))
    else:
        o_spec = pl.BlockSpec((tm, tn), lambda i, j, kk: (i, j))
    return pl.pallas_call(
        body, name=name, grid=(k // tm, n // tn, nk),
        in_specs=[pl.BlockSpec((tk, tm), lambda i, j, kk: (kk, i)), pl.BlockSpec((tk, tn), lambda i, j, kk: (kk, j))],
        out_specs=o_spec, out_shape=out_shape,
        compiler_params=pltpu.CompilerParams(dimension_semantics=("parallel", "parallel", "arbitrary")),
    )(a, b)


def linear(name, a, w, like):
    def fwd(a, w, like):
        return mm_nn(name + "_fwd", a, w), (a, w)

    def bwd(res, dy):
        a, w = res
        da = mm_nt(name + "_dx", dy, w)
        slots = (w.shape[0], w.shape[2]) if w.ndim == 3 else None
        dw = mm_tn(name + "_dw", a, dy, slots=slots)
        return da, jnp.zeros_like(w), dw

    return _op(fwd, bwd)(a, w, like)


def rowwise(name, f, rows, row_modes, row_diff, vecs, vec_kinds, vec_diff, out_defs, ncol=1, nctx=0):
    nr, nv = len(rows), len(vecs)
    t = rows[0].shape[0]
    tm = min(ROW_TILE, t)
    nrow = t // tm

    def row_spec(a, mode):
        w = a.shape[1]
        if mode == 'j':
            return pl.BlockSpec((tm, w // ncol), lambda i, j: (i, j))
        return pl.BlockSpec((tm, w), lambda i, j: (i, 0))

    def vec_spec(a, kind):
        nd = a.ndim
        if kind == 'shared':
            return pl.BlockSpec(a.shape, lambda i, j: (0,) * nd)
        return pl.BlockSpec((None,) + a.shape[1:], lambda i, j: ((i >= nctx).astype(jnp.int32),) + (0,) * (nd - 1))

    in_specs = [row_spec(a, m) for a, m in zip(rows, row_modes)] + [vec_spec(a, k) for a, k in zip(vecs, vec_kinds)]
    out_specs = [pl.BlockSpec((tm, w), lambda i, j: (i, j)) for w, _ in out_defs]
    out_shape = [jax.ShapeDtypeStruct((t, ncol * w), dt) for w, dt in out_defs]
    params = pltpu.CompilerParams(dimension_semantics=("arbitrary", "arbitrary"))

    def fwd_call(*args):
        def body(*refs):
            vals = [r[...] for r in refs[:nr + nv]]
            outs = f(*vals)
            for o_ref, o in zip(refs[nr + nv:], outs):
                o_ref[...] = o.astype(o_ref.dtype)
        return pl.pallas_call(body, name=name + "_fwd", grid=(nrow, ncol), in_specs=in_specs, out_specs=out_specs,
                              out_shape=out_shape, compiler_params=params)(*args)

    diff_idx = [i for i in range(nr) if row_diff[i]] + [nr + i for i in range(nv) if vec_diff[i]]
    d_rows = [i for i in range(nr) if row_diff[i]]
    d_vecs = [i for i in range(nv) if vec_diff[i]]

    def bwd_call(args, cts):
        n_in = nr + nv + len(out_defs)

        def body(*refs):
            i, j = pl.program_id(0), pl.program_id(1)
            vals = [r[...] for r in refs[:nr + nv]]
            ct = tuple(r[...] for r in refs[nr + nv:n_in])

            def g(*dvals):
                full = list(vals)
                for idx, v in zip(diff_idx, dvals):
                    full[idx] = v
                return tuple(f(*full))

            outs, vjp = jax.vjp(g, *[vals[idx] for idx in diff_idx])
            grads = vjp(tuple(c.astype(o.dtype) for c, o in zip(ct, outs)))
            out_refs = refs[n_in:]
            for p, _ in enumerate(d_rows):
                out_refs[p][...] = grads[p].astype(out_refs[p].dtype)
            for p, vi in enumerate(d_vecs):
                ref = out_refs[len(d_rows) + p]
                if vec_kinds[vi] == 'shared':
                    first = jnp.logical_and(i == 0, j == 0)
                else:
                    first = jnp.logical_and(jnp.logical_or(i == 0, i == nctx), j == 0)

                @pl.when(first)
                def _():
                    ref[...] = jnp.zeros_like(ref)
                ref[...] += grads[len(d_rows) + p].astype(F32)

        ct_specs = [pl.BlockSpec((tm, w), lambda i, j: (i, j)) for w, _ in out_defs]
        g_specs = [row_spec(rows[i], row_modes[i]) for i in d_rows] + [vec_spec(vecs[i], vec_kinds[i]) for i in d_vecs]
        g_shape = [jax.ShapeDtypeStruct(rows[i].shape, rows[i].dtype) for i in d_rows] + \
                  [jax.ShapeDtypeStruct(vecs[i].shape, F32) for i in d_vecs]
        return pl.pallas_call(body, name=name + "_bwd", grid=(nrow, ncol), in_specs=in_specs + ct_specs, out_specs=g_specs,
                              out_shape=g_shape, compiler_params=params)(*args, *cts)

    def fwd(*args):
        return tuple(fwd_call(*args)), args

    def bwd(args, cts):
        grads = bwd_call(args, cts)
        out = [None] * (nr + nv)
        for p, idx in enumerate(diff_idx):
            out[idx] = grads[p]
        for idx in range(nr + nv):
            if out[idx] is None:
                out[idx] = jnp.zeros_like(args[idx])
        return tuple(out)

    for i in d_rows:
        assert row_modes[i] == 'j' or ncol == 1
    return _op(fwd, bwd)(*rows, *vecs)


def _silu(x):
    return x * jax.nn.sigmoid(x)


def _roll_half(x):
    return pltpu.roll(x, HEAD_DIM // 2, axis=1)


@jax.custom_vjp
def _rope(x, cosf, sins):
    return x * cosf + _roll_half(x) * sins


def _rope_fwd(x, cosf, sins):
    return _rope(x, cosf, sins), (cosf, sins)


def _rope_bwd(res, dy):
    cosf, sins = res
    return dy * cosf + _roll_half(dy * sins), jnp.zeros_like(cosf), jnp.zeros_like(sins)


_rope.defvjp(_rope_fwd, _rope_bwd)


def _f_mod(x, shift, scale):
    return (x * (1.0 + scale) + shift,)


def _layer_norm(z, w, b):
    mu = jnp.mean(z, -1, keepdims=True)
    zc = z - mu
    var = jnp.mean(zc * zc, -1, keepdims=True)
    return zc * lax.rsqrt(var + EPS) * w + b


def _f_norm_mod(alpha, x, t, gate, w, b, shift, scale):
    xn = _layer_norm(alpha * x + gate * t, w, b)
    return xn, xn * (1.0 + scale) + shift


def _f_norm(alpha, x, t, gate, w, b):
    return (_layer_norm(alpha * x + gate * t, w, b),)


def _f_ret_prep(q, k, cosf, sins):
    return _rope(q, cosf, sins), _rope(k, cosf, sins) * HEAD_DIM ** -0.5


def _rms(x):
    return x * lax.rsqrt(jnp.mean(x * x, -1, keepdims=True) + EPS)


def _f_gated_out(of, ob, gate):
    return (_rms(of + ob) * _silu(gate),)


def _f_gated_out_w(of, ob, gate, w):
    return (_rms(of + ob) * w * _silu(gate),)


def _l2n(x):
    return x * lax.rsqrt(jnp.sum(x * x, -1, keepdims=True) + EPS)


def _f_dn_prep(cq, ck, cv):
    return _l2n(_silu(cq)) * HEAD_DIM ** -0.5, _l2n(_silu(ck)), _silu(cv)


def _f_qk_norm(x, cosf, sins, w):
    return (_rope(_rms(x) * w, cosf, sins),)


def _f_gates(ab, alog, dtb):
    tm = ab.shape[0]
    lane = lax.broadcasted_iota(jnp.int32, (1, LANE), 1)
    g = -jnp.exp(alog) * jax.nn.softplus(ab + dtb)
    beta = jax.nn.sigmoid(ab)
    r = lax.broadcasted_iota(jnp.int32, (tm, tm), 0)
    c = lax.broadcasted_iota(jnp.int32, (tm, tm), 1)
    same = (r // DN_CHUNK) == (c // DN_CHUNK)
    lower = jnp.where(jnp.logical_and(same, c <= r), 1.0, 0.0).astype(F32)
    upper = jnp.where(jnp.logical_and(same, c >= r), 1.0, 0.0).astype(F32)
    gl = jnp.dot(lower, g, precision=HI, preferred_element_type=F32)
    gu = jnp.dot(upper, g, precision=HI, preferred_element_type=F32)
    out = jnp.where(lane < DN_HEADS, gl, jnp.where(lane < 2 * DN_HEADS, gu, jnp.where(lane < 4 * DN_HEADS, beta, 0.0)))
    return (out,)


def swiglu(name, u, tn_cap=1408):
    t, f2 = u.shape
    ff = f2 // 2
    tn = _tile(ff, tn_cap)
    nc = ff // tn
    tm = min(ROW_TILE, t)

    def fwd_call(u):
        def body(g_ref, u_ref, o_ref):
            o_ref[...] = _silu(g_ref[...]) * u_ref[...]
        return pl.pallas_call(
            body, name=name + "_fwd", grid=(t // tm, nc),
            in_specs=[pl.BlockSpec((tm, tn), lambda i, j: (i, j)), pl.BlockSpec((tm, tn), lambda i, j: (i, j + nc))],
            out_specs=pl.BlockSpec((tm, tn), lambda i, j: (i, j)), out_shape=jax.ShapeDtypeStruct((t, ff), F32),
            compiler_params=pltpu.CompilerParams(dimension_semantics=("parallel", "parallel")))(u, u)

    def bwd_call(u, da):
        def body(own_ref, other_ref, da_ref, o_ref):
            j = pl.program_id(1)
            own, other, d = own_ref[...], other_ref[...], da_ref[...]
            sg = jax.nn.sigmoid(own)
            d_gate = d * other * (sg * (1.0 + own * (1.0 - sg)))
            d_up = d * _silu(other)
            o_ref[...] = jnp.where(j < nc, d_gate, d_up)
        return pl.pallas_call(
            body, name=name + "_bwd", grid=(t // tm, 2 * nc),
            in_specs=[pl.BlockSpec((tm, tn), lambda i, j: (i, j)), pl.BlockSpec((tm, tn), lambda i, j: (i, (j + nc) % (2 * nc))),
                      pl.BlockSpec((tm, tn), lambda i, j: (i, j % nc))],
            out_specs=pl.BlockSpec((tm, tn), lambda i, j: (i, j)), out_shape=jax.ShapeDtypeStruct((t, f2), F32),
            compiler_params=pltpu.CompilerParams(dimension_semantics=("parallel", "parallel")))(u, u, da)

    return _op(lambda u: (fwd_call(u), (u,)), lambda res, da: (bwd_call(res[0], da),))(u)


HALO = 8


def _conv_specs(t, c, tm, tc):
    nb8 = t // HALO
    per = tm // HALO
    cur = pl.BlockSpec((tm, tc), lambda j, i: (i, j))
    prev = pl.BlockSpec((HALO, tc), lambda j, i: (jnp.maximum(i * per - 1, 0), j))
    nxt = pl.BlockSpec((HALO, tc), lambda j, i: (jnp.minimum((i + 1) * per, nb8 - 1), j))
    return cur, prev, nxt


def _extended(prev_ref, cur_ref, next_ref, ext_ref, i, nrow, nctx, tm):
    has_prev = jnp.logical_and(i != 0, i != nctx)
    has_next = jnp.logical_and(i != nrow - 1, i != nctx - 1)
    ext_ref[0:HALO, :] = jnp.where(has_prev, prev_ref[...], 0.0)
    ext_ref[HALO:HALO + tm, :] = cur_ref[...]
    ext_ref[HALO + tm:, :] = jnp.where(has_next, next_ref[...], 0.0)


def _conv_call(name, x, w8, nctx, flip):
    t, c = x.shape
    tm = min(ROW_TILE, t)
    tc = _tile(c, 512)
    nrow = t // tm
    pad = DN_CONV_K // 2

    def body(cur_ref, prev_ref, next_ref, w_ref, o_ref, ext_ref):
        i = pl.program_id(1)
        _extended(prev_ref, cur_ref, next_ref, ext_ref, i, nrow, nctx, tm)
        acc = jnp.zeros((tm, tc), F32)
        for j in range(DN_CONV_K):
            wj = w_ref[(DN_CONV_K - 1 - j) if flip else j, :][None, :]
            acc = acc + wj * ext_ref[HALO - pad + j:HALO - pad + j + tm, :]
        o_ref[...] = acc

    cur, prev, nxt = _conv_specs(t, c, tm, tc)
    return pl.pallas_call(
        body, name=name, grid=(c // tc, nrow),
        in_specs=[cur, prev, nxt, pl.BlockSpec((8, tc), lambda j, i: (0, j))],
        out_specs=pl.BlockSpec((tm, tc), lambda j, i: (i, j)), out_shape=jax.ShapeDtypeStruct((t, c), F32),
        scratch_shapes=[pltpu.VMEM((tm + 2 * HALO, tc), F32)],
        compiler_params=pltpu.CompilerParams(dimension_semantics=("arbitrary", "arbitrary")))(x, x, x, w8)


def _conv_dw_call(name, x, dy, nctx):
    t, c = x.shape
    tm = min(ROW_TILE, t)
    tc = _tile(c, 512)
    nrow = t // tm
    pad = DN_CONV_K // 2

    def body(cur_ref, prev_ref, next_ref, dy_ref, o_ref, ext_ref):
        i = pl.program_id(1)
        _extended(prev_ref, cur_ref, next_ref, ext_ref, i, nrow, nctx, tm)

        @pl.when(i == 0)
        def _():
            o_ref[...] = jnp.zeros_like(o_ref)
        dy = dy_ref[...]
        rows = [jnp.sum(dy * ext_ref[HALO - pad + j:HALO - pad + j + tm, :], axis=0, keepdims=True) for j in range(DN_CONV_K)]
        rows += [jnp.zeros((1, tc), F32)] * (8 - DN_CONV_K)
        o_ref[...] += jnp.concatenate(rows, axis=0)

    cur, prev, nxt = _conv_specs(t, c, tm, tc)
    return pl.pallas_call(
        body, name=name, grid=(c // tc, nrow),
        in_specs=[cur, prev, nxt, pl.BlockSpec((tm, tc), lambda j, i: (i, j))],
        out_specs=pl.BlockSpec((8, tc), lambda j, i: (0, j)), out_shape=jax.ShapeDtypeStruct((8, c), F32),
        scratch_shapes=[pltpu.VMEM((tm + 2 * HALO, tc), F32)],
        compiler_params=pltpu.CompilerParams(dimension_semantics=("arbitrary", "arbitrary")))(x, x, x, dy)


def short_conv(name, x, w8, nctx):
    def fwd(x, w8):
        return _conv_call(name + "_fwd", x, w8, nctx, False), (x, w8)

    def bwd(res, dy):
        x, w8 = res
        return _conv_call(name + "_dx", dy, w8, nctx, True), _conv_dw_call(name + "_dw", x, dy, nctx)

    return _op(fwd, bwd)(x, w8)


def _mm_raw(a, b, form, exact):
    dims = {"nn": ((2,), (1,)), "nt": ((2,), (2,)), "tn": ((1,), (1,))}[form]
    return lax.dot_general(_cast(a, exact), _cast(b, exact), (dims, ((0,), (0,))),
                           precision=HI if exact else None, preferred_element_type=F32)


@functools.partial(jax.custom_vjp, nondiff_argnums=(2, 3))
def _mm(a, b, form, exact=False):
    return _mm_raw(a, b, form, exact)


def _mm_fwd(a, b, form, exact):
    return _mm_raw(a, b, form, exact), (a, b)


def _mm_bwd(form, exact, res, dc):
    a, b = res
    if form == "nn":
        return _mm_raw(dc, b, "nt", exact), _mm_raw(a, dc, "tn", exact)
    if form == "nt":
        return _mm_raw(dc, b, "nn", exact), _mm_raw(dc, a, "tn", exact)
    return _mm_raw(b, dc, "nt", exact), _mm_raw(a, dc, "nn", exact)


_mm.defvjp(_mm_fwd, _mm_bwd)


def chunk_scan(name, f, chunk, heads, ncc, q, k, v, gargs_f, gargs_b, pargs):
    t = q.shape[0]
    nc = t // chunk
    g_all = 2 * heads
    d = HEAD_DIM
    hw = heads * d
    ng, npar = len(gargs_f), len(pargs)

    def pos_b(n):
        return jnp.where(n < ncc, ncc - 1 - n, ncc + nc - 1 - n)

    def specs(rev_visit):
        def vis(n):
            return (nc - 1 - n) if rev_visit else n
        qf = pl.BlockSpec((chunk, hw), lambda n: (vis(n), 0))
        qb = pl.BlockSpec((chunk, hw), lambda n: (pos_b(vis(n)), 0))
        gf = [pl.BlockSpec((heads, None) + a.shape[2:], lambda n: (0, vis(n), 0, 0)) for a in gargs_f]
        gb = [pl.BlockSpec((heads, None) + a.shape[2:], lambda n: (0, pos_b(vis(n)), 0, 0)) for a in gargs_b]
        ps = [pl.BlockSpec(a.shape, lambda n: (0, 0, 0)) for a in pargs]
        ss = pl.BlockSpec((g_all, None, d, d), lambda n: (0, vis(n), 0, 0))
        return qf, qb, gf, gb, ps, ss

    params = pltpu.CompilerParams(dimension_semantics=("arbitrary",))
    n_in = 6 + 2 * ng + npar

    def cols(h):
        return slice(h * d, (h + 1) * d)

    def stacked_inputs(refs):
        qkv = [jnp.stack([refs[i][:, cols(h)] for h in range(heads)] + [refs[3 + i][:, cols(h)] for h in range(heads)])
               for i in range(3)]
        gar = [jnp.concatenate([refs[6 + i][...], refs[6 + ng + i][...]], axis=0) for i in range(ng)]
        par = [r[...] for r in refs[6 + 2 * ng:n_in]]
        return qkv + gar + par

    def scatter(vals, refs_f, refs_b):
        for r_f, r_b, val in zip(refs_f, refs_b, vals):
            for h in range(heads):
                r_f[:, cols(h)] = val[h]
                r_b[:, cols(h)] = val[heads + h]

    def fwd_call(q, k, v, *rest):
        qf, qb, gf, gb, ps, ss = specs(False)

        def body(*refs):
            of_ref, ob_ref, sp_ref, s_ref = refs[n_in:]

            @pl.when(pl.program_id(0) == 0)
            def _():
                s_ref[...] = jnp.zeros_like(s_ref)
            s_prev = s_ref[...]
            sp_ref[...] = s_prev
            o, s_new = f(heads, *stacked_inputs(refs), s_prev)
            scatter([o], [of_ref], [ob_ref])
            s_ref[...] = s_new

        oshape = jax.ShapeDtypeStruct((t, hw), F32)
        return pl.pallas_call(
            body, name=name + "_fwd", grid=(nc,), in_specs=[qf] * 3 + [qb] * 3 + gf + gb + ps, out_specs=[qf, qb, ss],
            out_shape=[oshape, oshape, jax.ShapeDtypeStruct((g_all, nc, d, d), F32)],
            scratch_shapes=[pltpu.VMEM((g_all, d, d), F32)], compiler_params=params)(q, k, v, q, k, v, *rest)

    def bwd_call(q, k, v, rest, s_prev, do_f, do_b):
        qf, qb, gf, gb, ps, ss = specs(True)

        def body(*refs):
            sp_ref, dof_ref, dob_ref = refs[n_in:n_in + 3]
            outs = refs[n_in + 3:]
            dqkv_f, dqkv_b = outs[0:3], outs[3:6]
            dg_f, dg_b = outs[6:6 + ng], outs[6 + ng:6 + 2 * ng]
            dp_refs = outs[6 + 2 * ng:6 + 2 * ng + npar]
            ds_ref = outs[6 + 2 * ng + npar]

            @pl.when(pl.program_id(0) == 0)
            def _():
                ds_ref[...] = jnp.zeros_like(ds_ref)
                for r in dp_refs:
                    r[...] = jnp.zeros_like(r)
            _, vjp = jax.vjp(functools.partial(f, heads), *stacked_inputs(refs), sp_ref[...])
            do = jnp.stack([dof_ref[:, cols(h)] for h in range(heads)] + [dob_ref[:, cols(h)] for h in range(heads)])
            grads = vjp((do, ds_ref[...]))
            scatter(grads[:3], dqkv_f, dqkv_b)
            for r_f, r_b, gr in zip(dg_f, dg_b, grads[3:3 + ng]):
                r_f[...] = gr[:heads]
                r_b[...] = gr[heads:]
            for r, gr in zip(dp_refs, grads[3 + ng:3 + ng + npar]):
                r[...] += gr
            ds_ref[...] = grads[3 + ng + npar]

        dshape = jax.ShapeDtypeStruct((t, hw), F32)
        return pl.pallas_call(
            body, name=name + "_bwd", grid=(nc,), in_specs=[qf] * 3 + [qb] * 3 + gf + gb + ps + [ss, qf, qb],
            out_specs=[qf] * 3 + [qb] * 3 + gf + gb + ps,
            out_shape=[dshape] * 6 + [jax.ShapeDtypeStruct(a.shape, F32) for a in list(gargs_f) + list(gargs_b) + list(pargs)],
            scratch_shapes=[pltpu.VMEM((g_all, d, d), F32)], compiler_params=params)(q, k, v, q, k, v, *rest, s_prev, do_f, do_b)

    def fwd(q, k, v, *rest):
        o_f, o_b, s_prev = fwd_call(q, k, v, *rest)
        return (o_f, o_b), (q, k, v, rest, s_prev)

    def bwd(res, do):
        q, k, v, rest, s_prev = res
        grads = bwd_call(q, k, v, rest, s_prev, do[0], do[1])
        return tuple(grads[i] + grads[3 + i] for i in range(3)) + tuple(grads[6:])

    return _op(fwd, bwd)(q, k, v, *gargs_f, *gargs_b, *pargs)


def _order_masks(heads, c):
    g = 2 * heads
    rev = lax.broadcasted_iota(jnp.int32, (g, c, c), 0) >= heads
    i = lax.broadcasted_iota(jnp.int32, (g, c, c), 1)
    j = lax.broadcasted_iota(jnp.int32, (g, c, c), 2)
    rel = jnp.where(rev, j - i, i - j)
    pos = lax.broadcasted_iota(jnp.int32, (g, c, 1), 1)
    p = jnp.where(lax.broadcasted_iota(jnp.int32, (g, c, 1), 0) >= heads, c - 1 - pos, pos)
    return rel, p


def _f_retention(heads, q, k, v, logit, s):
    c = RET_CHUNK
    lg = jax.nn.log_sigmoid(logit)
    rel, p = _order_masks(heads, c)
    rel = rel.astype(F32)
    p = p.astype(F32)
    decay = jnp.where(rel >= 0, jnp.exp(jnp.maximum(rel, 0.0) * lg), 0.0)
    o_intra = _mm(_mm(q, k, "nt") * decay, v, "nn")
    q_decay = jnp.exp((p + 1.0) * lg)
    k_decay = jnp.exp((c - 1.0 - p) * lg)
    o_inter = _mm(q * q_decay, s, "nn")
    s_new = s * jnp.exp(c * lg) + _mm(k * k_decay, v, "tn")
    return o_intra + o_inter, s_new


def _f_delta(heads, q, k, v, gc, gr, bc, s):
    c = DN_CHUNK
    rel, p = _order_masks(heads, c)
    tri = rel >= 0
    strict = rel > 0
    decay = jnp.where(tri, jnp.exp(jnp.where(tri, gc - gr, 0.0)), 0.0)
    kb = k * bc
    vb = v * bc
    a = jnp.where(strict, _mm(kb, k, "nt") * decay, 0.0)
    pw = -a
    tinv = jnp.where(rel == 0, 1.0, 0.0).astype(F32) + pw
    for _ in range(int(math.log2(c)) - 1):
        pw = _mm(pw, pw, "nn", True)
        tinv = tinv + _mm(tinv, pw, "nn", True)
    eg = jnp.exp(gc)
    w_val = _mm(tinv, vb, "nn")
    k_cum = _mm(tinv, kb * eg, "nn")
    qk = _mm(q, k, "nt") * decay
    g_last = jnp.sum(jnp.where(p == c - 1, gc, 0.0), axis=1, keepdims=True)
    k_g = k * jnp.exp(g_last - gc)
    v_new = w_val - _mm(k_cum, s, "nn")
    o = _mm(q * eg, s, "nn") + _mm(qk, v_new, "nn")
    s_new = s * jnp.exp(g_last) + _mm(k_g, v_new, "tn")
    return o, s_new


def _att_tiles(tq, tk):
    return _tile(tq, 1024, 8), _tile(tk, 1408, 8)


def _att_fwd_call(name, q, k, v):
    tq, tk = q.shape[0], k.shape[0]
    h, kvh = q.shape[1] // HEAD_DIM, k.shape[1] // HEAD_DIM
    grp = h // kvh
    bq, bk = _att_tiles(tq, tk)
    nk = tk // bk
    scale = HEAD_DIM ** -0.5

    def body(q_ref, k_ref, v_ref, o_ref, lse_ref, m_sc, l_sc, acc_sc, q2_sc):
        kj = pl.program_id(2)

        @pl.when(kj == 0)
        def _():
            m_sc[...] = jnp.full_like(m_sc, -jnp.inf)
            l_sc[...] = jnp.zeros_like(l_sc)
            acc_sc[...] = jnp.zeros_like(acc_sc)
            q2_sc[...] = (q_ref[...] * (scale * LOG2E)).astype(q2_sc.dtype)
        s = _dot(q2_sc[...], k_ref[...], ((1,), (1,)), False)
        m_prev = m_sc[...]
        m_new = jnp.maximum(m_prev, jnp.max(s, axis=1, keepdims=True))
        p = jnp.exp2(s - m_new)
        alpha = jnp.exp2(m_prev - m_new)
        l_sc[...] = alpha * l_sc[...] + jnp.sum(p, axis=1, keepdims=True)
        acc_sc[...] = alpha * acc_sc[...] + _dot(p, v_ref[...], ((1,), (0,)), False)
        m_sc[...] = m_new

        @pl.when(kj == nk - 1)
        def _():
            o_ref[...] = acc_sc[...] / l_sc[...]
            lse_ref[...] = m_sc[...] * LN2 + jnp.log(l_sc[...])

    return pl.pallas_call(
        body, name=name, grid=(h, tq // bq, nk),
        in_specs=[pl.BlockSpec((bq, HEAD_DIM), lambda hh, i, j: (i, hh)),
                  pl.BlockSpec((bk, HEAD_DIM), lambda hh, i, j: (j, hh // grp)),
                  pl.BlockSpec((bk, HEAD_DIM), lambda hh, i, j: (j, hh // grp))],
        out_specs=[pl.BlockSpec((bq, HEAD_DIM), lambda hh, i, j: (i, hh)),
                   pl.BlockSpec((None, bq, 1), lambda hh, i, j: (hh, i, 0))],
        out_shape=[jax.ShapeDtypeStruct((tq, h * HEAD_DIM), F32), jax.ShapeDtypeStruct((h, tq, 1), F32)],
        scratch_shapes=[pltpu.VMEM((bq, 1), F32), pltpu.VMEM((bq, 1), F32), pltpu.VMEM((bq, HEAD_DIM), F32),
                        pltpu.VMEM((bq, HEAD_DIM), BF16)],
        compiler_params=pltpu.CompilerParams(dimension_semantics=("parallel", "parallel", "arbitrary")))(q, k, v)


def _att_dq_call(name, q, k, v, o, lse, do):
    tq, tk = q.shape[0], k.shape[0]
    h, kvh = q.shape[1] // HEAD_DIM, k.shape[1] // HEAD_DIM
    grp = h // kvh
    bq, bk = _att_tiles(tq, tk)
    nk = tk // bk
    scale = HEAD_DIM ** -0.5

    def body(q_ref, k_ref, v_ref, o_ref, lse_ref, do_ref, dq_ref, delta_sc, lse2_sc, q2_sc):
        kj = pl.program_id(2)

        @pl.when(kj == 0)
        def _():
            dq_ref[...] = jnp.zeros_like(dq_ref)
            delta_sc[...] = jnp.sum(do_ref[...] * o_ref[...], axis=1, keepdims=True)
            lse2_sc[...] = lse_ref[...] * LOG2E
            q2_sc[...] = (q_ref[...] * (scale * LOG2E)).astype(q2_sc.dtype)
        s = _dot(q2_sc[...], k_ref[...], ((1,), (1,)), False)
        p = jnp.exp2(s - lse2_sc[...])
        dp = _dot(do_ref[...], v_ref[...], ((1,), (1,)), False)
        ds = p * (dp - delta_sc[...])
        dq_ref[...] += _dot(ds, k_ref[...], ((1,), (0,)), False) * scale

    qspec = pl.BlockSpec((bq, HEAD_DIM), lambda hh, i, j: (i, hh))
    kspec = pl.BlockSpec((bk, HEAD_DIM), lambda hh, i, j: (j, hh // grp))
    return pl.pallas_call(
        body, name=name, grid=(h, tq // bq, nk),
        in_specs=[qspec, kspec, kspec, qspec, pl.BlockSpec((None, bq, 1), lambda hh, i, j: (hh, i, 0)), qspec],
        out_specs=qspec, out_shape=jax.ShapeDtypeStruct(q.shape, F32),
        scratch_shapes=[pltpu.VMEM((bq, 1), F32), pltpu.VMEM((bq, 1), F32), pltpu.VMEM((bq, HEAD_DIM), BF16)],
        compiler_params=pltpu.CompilerParams(dimension_semantics=("parallel", "parallel", "arbitrary")))(q, k, v, o, lse, do)


def _att_dkv_call(name, q, k, v, o, lse, do):
    tq, tk = q.shape[0], k.shape[0]
    h, kvh = q.shape[1] // HEAD_DIM, k.shape[1] // HEAD_DIM
    grp = h // kvh
    bq, bk = _att_tiles(tq, tk)
    nq = tq // bq
    nr = grp * nq
    scale = HEAD_DIM ** -0.5

    def body(q_ref, k_ref, v_ref, o_ref, lse_ref, do_ref, dk_ref, dv_ref):
        r = pl.program_id(2)

        @pl.when(r == 0)
        def _():
            dk_ref[...] = jnp.zeros_like(dk_ref)
            dv_ref[...] = jnp.zeros_like(dv_ref)
        q2 = _cast(q_ref[...] * (scale * LOG2E), False)
        s = _dot(q2, k_ref[...], ((1,), (1,)), False)
        p = jnp.exp2(s - lse_ref[...] * LOG2E)
        dv_ref[...] += _dot(p, do_ref[...], ((0,), (0,)), False)
        dp = _dot(do_ref[...], v_ref[...], ((1,), (1,)), False)
        delta = jnp.sum(do_ref[...] * o_ref[...], axis=1, keepdims=True)
        ds = p * (dp - delta)
        dk_ref[...] += _dot(ds, q2, ((0,), (0,)), False) * LN2

    qspec = pl.BlockSpec((bq, HEAD_DIM), lambda kh, j, r: (r % nq, kh * grp + r // nq))
    kspec = pl.BlockSpec((bk, HEAD_DIM), lambda kh, j, r: (j, kh))
    return pl.pallas_call(
        body, name=name, grid=(kvh, tk // bk, nr),
        in_specs=[qspec, kspec, kspec, qspec, pl.BlockSpec((None, bq, 1), lambda kh, j, r: (kh * grp + r // nq, r % nq, 0)), qspec],
        out_specs=[kspec, kspec], out_shape=[jax.ShapeDtypeStruct(k.shape, F32), jax.ShapeDtypeStruct(v.shape, F32)],
        compiler_params=pltpu.CompilerParams(dimension_semantics=("parallel", "parallel", "arbitrary")))(q, k, v, o, lse, do)


def attention(name, q, k, v):
    def fwd(q, k, v):
        o, lse = _att_fwd_call(name + "_fwd", q, k, v)
        return o, (q, k, v, o, lse)

    def bwd(res, do):
        q, k, v, o, lse = res
        dq = _att_dq_call(name + "_dq", q, k, v, o, lse, do)
        dk, dv = _att_dkv_call(name + "_dkv", q, k, v, o, lse, do)
        return dq, dk, dv

    return _op(fwd, bwd)(q, k, v)


def loss_head(y, target):
    t, d = y.shape
    tm = min(ROW_TILE, t)

    def fwd(y, target):
        def body(y_ref, t_ref, l_ref, dy_ref):
            @pl.when(pl.program_id(0) == 0)
            def _():
                l_ref[...] = jnp.zeros_like(l_ref)
            err = y_ref[...] - t_ref[...]
            row = jnp.sum(err * err, axis=1, keepdims=True) * (1.0 / d)
            l_ref[...] += 0.5 * jnp.sum(row, axis=0, keepdims=True)
            dy_ref[...] = err * (1.0 / d)
        loss, dy = pl.pallas_call(
            body, name="loss_head", grid=(t // tm,),
            in_specs=[pl.BlockSpec((tm, d), lambda i: (i, 0)), pl.BlockSpec((tm, d), lambda i: (i, 0))],
            out_specs=[pl.BlockSpec((1, 1), lambda i: (0, 0)), pl.BlockSpec((tm, d), lambda i: (i, 0))],
            out_shape=[jax.ShapeDtypeStruct((1, 1), F32), jax.ShapeDtypeStruct((t, d), F32)],
            compiler_params=pltpu.CompilerParams(dimension_semantics=("arbitrary",)))(y, target)
        return loss[0, 0], (dy,)

    def bwd(res, g):
        return res[0] * g, jnp.zeros_like(res[0])

    return _op(fwd, bwd)(y, target)


def _my_index():
    return 4 * lax.axis_index("x") + 2 * lax.axis_index("y") + lax.axis_index("c")


def exchange(name, arrays, modes):
    n = len(arrays)
    out_shape = [jax.ShapeDtypeStruct(((N_DEV,) + a.shape) if m == 'gather' else a.shape, a.dtype) for a, m in zip(arrays, modes)]

    def body(*refs):
        ins, outs = refs[:n], refs[n:2 * n]
        send_sems, recv_sems, local_sems = refs[2 * n:]
        x, y, c = lax.axis_index("x"), lax.axis_index("y"), lax.axis_index("c")
        me = 4 * x + 2 * y + c
        copies = []
        for a in range(n):
            gather = modes[a] == 'gather'
            loc = pltpu.make_async_copy(ins[a] if gather else ins[a].at[me], outs[a].at[me], local_sems.at[a])
            loc.start()
            copies.append(loc)
            for dist in range(1, N_DEV):
                px = (1 - x) if dist & 4 else x
                py = (1 - y) if dist & 2 else y
                pc = (1 - c) if dist & 1 else c
                peer = 4 * px + 2 * py + pc
                cp = pltpu.make_async_remote_copy(
                    src_ref=ins[a] if gather else ins[a].at[peer], dst_ref=outs[a].at[me],
                    send_sem=send_sems.at[a * (N_DEV - 1) + dist - 1], recv_sem=recv_sems.at[a * (N_DEV - 1) + dist - 1],
                    device_id=(px, py, pc), device_id_type=MESH)
                cp.start()
                copies.append(cp)
        for cp in copies:
            cp.wait()

    hbm = pl.BlockSpec(memory_space=pl.ANY)
    return pl.pallas_call(
        body, name=name, in_specs=[hbm] * n, out_specs=[hbm] * n, out_shape=out_shape,
        scratch_shapes=[pltpu.SemaphoreType.DMA((n * (N_DEV - 1),)), pltpu.SemaphoreType.DMA((n * (N_DEV - 1),)),
                        pltpu.SemaphoreType.DMA((n,))],
        compiler_params=pltpu.CompilerParams(has_side_effects=True))(*arrays)


def _adamw_math(w, g, m, v):
    m = ADAM_B1 * m + (1.0 - ADAM_B1) * g
    v = ADAM_B2 * v + (1.0 - ADAM_B2) * (g * g)
    m_hat = m / (1.0 - ADAM_B1 ** ADAM_STEP)
    v_hat = v / (1.0 - ADAM_B2 ** ADAM_STEP)
    delta = -ADAM_LR * (m_hat / (jnp.sqrt(v_hat) + ADAM_EPS) + ADAM_WD * w)
    return delta, m, v


def adamw(name, parts, w, m, v, scale=None, row_bytes_cap=1 << 20):
    p, r, c = parts.shape
    tr = _tile(r, max(8, row_bytes_cap // (4 * c) // 8 * 8), 8)
    extra = [] if scale is None else [scale]

    def body(p_ref, *refs):
        w_ref, m_ref, v_ref, g_ref, d_ref, nm_ref, nv_ref = refs[len(extra):]
        g = p_ref[0]
        for q in range(1, p):
            g = g + p_ref[q]
        if extra:
            g = g * refs[0][...]
        delta, nm, nv = _adamw_math(w_ref[...], g, m_ref[...], v_ref[...])
        g_ref[...] = g
        d_ref[...] = delta
        nm_ref[...] = nm
        nv_ref[...] = nv

    spec = pl.BlockSpec((tr, c), lambda i: (i, 0))
    shape = jax.ShapeDtypeStruct((r, c), F32)
    return pl.pallas_call(
        body, name=name, grid=(r // tr,),
        in_specs=[pl.BlockSpec((p, tr, c), lambda i: (0, i, 0))] + [spec] * (3 + len(extra)),
        out_specs=[spec] * 4, out_shape=[shape] * 4,
        compiler_params=pltpu.CompilerParams(dimension_semantics=("parallel",)))(parts, *extra, w, m, v)


def _rope_tables(n_lat, n_ctx):
    rows = n_lat // GRID_W
    row = jnp.repeat(jnp.arange(rows, dtype=F32), GRID_W)
    col = jnp.tile(jnp.arange(GRID_W, dtype=F32), rows)
    n_freq = HEAD_DIM // 4
    inv = ROPE_THETA ** (-jnp.arange(n_freq, dtype=F32) / n_freq)
    ang = jnp.concatenate([row[:, None] * inv, col[:, None] * inv], -1)
    cos, sin = jnp.cos(ang), jnp.sin(ang)
    cosf = jnp.concatenate([jnp.ones((n_ctx, HEAD_DIM), F32), jnp.concatenate([cos, cos], -1)], 0)
    sins = jnp.concatenate([jnp.zeros((n_ctx, HEAD_DIM), F32), jnp.concatenate([-sin, sin], -1)], 0)
    return cosf, sins


def _widths():
    rw, dw, aw, kw = RET_HEADS * HEAD_DIM, DN_HEADS * HEAD_DIM, ATT_HEADS * HEAD_DIM, ATT_KV_HEADS * HEAD_DIM
    return rw, dw, aw, kw


def _pad_in_proj(full):
    rw, dw, aw, kw = _widths()
    n_ab = 4 * DN_HEADS
    a0 = 4 * rw + 4 * dw
    main = jnp.concatenate([full[:, :a0], full[:, a0 + n_ab:]], 1)
    ab = jnp.concatenate([full[:, a0:a0 + n_ab], jnp.zeros((full.shape[0], LANE - n_ab), full.dtype)], 1)
    return jnp.concatenate([main, ab], 1)


def _unpad_in_proj(dpad):
    rw, dw, aw, kw = _widths()
    n_ab = 4 * DN_HEADS
    a0 = 4 * rw + 4 * dw
    main_w = dpad.shape[1] - LANE
    return jnp.concatenate([dpad[:, :a0], dpad[:, main_w:main_w + n_ab], dpad[:, a0:main_w]], 1)


def _chunked(a, chunk):
    t, g = a.shape
    at = a.T.reshape(g, t // chunk, chunk)
    return at[..., None], at[:, :, None, :]


def _mixer(i, h, wpad, like_in, sm, cosf, sins, n_ctx, tag):
    rw, dw, aw, kw = _widths()
    t = h.shape[0]
    nctx = n_ctx // min(ROW_TILE, t)
    p = linear(f"in_proj{tag}", h, wpad, like_in)
    o = 0
    rq, rk, rv, rg = (p[:, o + n * rw:o + (n + 1) * rw] for n in range(4))
    o += 4 * rw
    dqkv, dz = p[:, o:o + 3 * dw], p[:, o + 3 * dw:o + 4 * dw]
    o += 4 * dw
    aq, ak, av = p[:, o:o + aw], p[:, o + aw:o + aw + kw], p[:, o + aw + kw:o + aw + 2 * kw]
    ab = p[:, o + aw + 2 * kw:]

    qr, kr = rowwise(f"ret_prep{tag}", _f_ret_prep, [rq, rk, cosf, sins], ['j', 'j', 'b', 'b'], [True, True, False, False],
                     [], [], [], [(HEAD_DIM, F32)] * 2, ncol=RET_HEADS, nctx=nctx)
    logit = sm["ret_decay_logit"][i].reshape(2 * RET_HEADS, 1, 1)
    o_ret = chunk_scan(f"ret_scan{tag}", _f_retention, RET_CHUNK, RET_HEADS, n_ctx // RET_CHUNK, qr, kr, rv, [], [], [logit])
    y_ret, = rowwise(f"ret_out{tag}", _f_gated_out, [o_ret[0], o_ret[1], rg], ['j'] * 3, [True] * 3, [], [], [],
                     [(HEAD_DIM, F32)], ncol=RET_HEADS, nctx=nctx)

    conv = short_conv(f"dn_conv{tag}", dqkv, sm["conv_w8"][i], nctx)
    dq, dk, dv = rowwise(f"dn_prep{tag}", _f_dn_prep, [conv[:, :dw], conv[:, dw:2 * dw], conv[:, 2 * dw:]], ['j'] * 3, [True] * 3,
                         [], [], [], [(HEAD_DIM, F32)] * 3, ncol=DN_HEADS, nctx=nctx)
    pad8 = lambda a: jnp.concatenate([a.reshape(1, 2 * DN_HEADS), jnp.zeros((1, LANE - 2 * DN_HEADS), F32)], 1)
    gb, = rowwise(f"dn_gates{tag}", _f_gates, [ab], ['j'], [True], [pad8(sm["dn_a_log"][i]), pad8(sm["dn_dt_bias"][i])],
                  ['shared'] * 2, [True] * 2, [(LANE, F32)], nctx=nctx)
    g_col, g_row = _chunked(gb[:, :2 * DN_HEADS], DN_CHUNK)
    b_col, _ = _chunked(gb[:, 2 * DN_HEADS:4 * DN_HEADS], DN_CHUNK)
    o_dn = chunk_scan(f"dn_scan{tag}", _f_delta, DN_CHUNK, DN_HEADS, n_ctx // DN_CHUNK, dq, dk, dv,
                      [a[:DN_HEADS] for a in (g_col, g_row, b_col)], [a[DN_HEADS:] for a in (g_col, g_row, b_col)], [])
    y_dn, = rowwise(f"dn_out{tag}", _f_gated_out_w, [o_dn[0], o_dn[1], dz], ['j'] * 3, [True] * 3,
                    [sm["dn_norm_w"][i].reshape(1, HEAD_DIM)], ['shared'], [True], [(HEAD_DIM, F32)], ncol=DN_HEADS, nctx=nctx)

    qn, = rowwise(f"att_qn{tag}", _f_qk_norm, [aq, cosf, sins], ['j', 'b', 'b'], [True, False, False],
                  [sm["att_qn_w"][i].reshape(1, HEAD_DIM)], ['shared'], [True], [(HEAD_DIM, F32)], ncol=ATT_HEADS, nctx=nctx)
    kn, = rowwise(f"att_kn{tag}", _f_qk_norm, [ak, cosf, sins], ['j', 'b', 'b'], [True, False, False],
                  [sm["att_kn_w"][i].reshape(1, HEAD_DIM)], ['shared'], [True], [(HEAD_DIM, F32)], ncol=ATT_KV_HEADS, nctx=nctx)
    y_att_lat = attention(f"att_lat{tag}", qn[n_ctx:], kn, av)
    return y_ret, y_dn, y_att_lat, (qn, kn, av)


def _forward(x, ctx, target, mods, wts, likes, sm):
    depth = mods.shape[0]
    alpha = (2 * depth) ** 0.25
    n_ctx, d = ctx.shape
    n_lat = x.shape[0]
    cosf, sins = _rope_tables(n_lat, n_ctx)
    xs = jnp.concatenate([ctx, x], 0)
    for i in range(depth):
        last = i == depth - 1
        tag = f"_{i}"
        t = xs.shape[0]
        nctx = n_ctx // min(ROW_TILE, t)
        seg = lambda kk: mods[i, :, kk][:, None, :]
        lat = lambda kk: mods[i, 1, kk][None, :]
        h, = rowwise(f"mod1{tag}", _f_mod, [xs], ['j'], [True], [seg(0), seg(1)], ['seg'] * 2, [True] * 2, [(d, F32)], nctx=nctx)
        y_ret, y_dn, y_att_lat, (qn, kn, av) = _mixer(i, h, wts["w_in"][i], likes["w_in"][i], sm, cosf, sins, n_ctx, tag)
        vec = lambda a: a.reshape(1, d)
        ln1 = [vec(sm["ln1_w"][i]), vec(sm["ln1_b"][i])]
        ln2 = [vec(sm["ln2_w"][i]), vec(sm["ln2_b"][i])]
        if last:
            y = jnp.concatenate([y_ret[n_ctx:], y_dn[n_ctx:], y_att_lat], 1)
            xs = xs[n_ctx:]
            nctx = 0
            kinds = ['shared']
            g1, sh2, sc2, g2 = lat(2), lat(3), lat(4), lat(5)
        else:
            y_att_ctx = attention(f"att_ctx{tag}", qn[:n_ctx], kn[:n_ctx], av[:n_ctx])
            y = jnp.concatenate([y_ret, y_dn, jnp.concatenate([y_att_ctx, y_att_lat], 0)], 1)
            kinds = ['seg']
            g1, sh2, sc2, g2 = seg(2), seg(3), seg(4), seg(5)
        tt = linear(f"out_proj{tag}", y, wts["w_o"][i], likes["w_o"][i])
        x1, h2 = rowwise(f"norm1{tag}", functools.partial(_f_norm_mod, alpha), [xs, tt], ['j', 'j'], [True, True],
                         [g1] + ln1 + [sh2, sc2], kinds + ['shared'] * 2 + kinds * 2, [True] * 5, [(d, F32)] * 2, nctx=nctx)
        u = linear(f"ffn_in{tag}", h2, wts["w_ffn_in"][i], likes["w_ffn_in"][i])
        act = swiglu(f"swiglu{tag}", u)
        t2 = linear(f"ffn_out{tag}", act, wts["w_ffn_out"][i], likes["w_ffn_out"][i])
        xs, = rowwise(f"norm2{tag}", functools.partial(_f_norm, alpha), [x1, t2], ['j', 'j'], [True, True],
                      [g2] + ln2, kinds + ['shared'] * 2, [True] * 3, [(d, F32)], nctx=nctx)
    return loss_head(xs, target)


def _ada_forward(cond16, w_ada, b_loc):
    depth, d, n = w_ada.shape
    tn = _tile(n, 512)

    def body(c_ref, w_ref, b_ref, o_ref):
        o_ref[...] = jnp.dot(c_ref[...], w_ref[...], precision=HI, preferred_element_type=F32) + b_ref[...]

    return pl.pallas_call(
        body, name="ada_fwd", grid=(depth, n // tn),
        in_specs=[pl.BlockSpec((16, d), lambda l, j: (0, 0)), pl.BlockSpec((None, d, tn), lambda l, j: (l, 0, j)),
                  pl.BlockSpec((None, 1, tn), lambda l, j: (l, 0, j))],
        out_specs=pl.BlockSpec((None, 16, tn), lambda l, j: (l, 0, j)), out_shape=jax.ShapeDtypeStruct((depth, 16, n), F32),
        compiler_params=pltpu.CompilerParams(dimension_semantics=("parallel", "parallel")))(cond16, w_ada, b_loc)


def _ada_backward(cond16, w_ada, dm16):
    depth, d, n = w_ada.shape
    tn = _tile(n, 512)

    def body(c_ref, w_ref, dm_ref, gw_ref, dc_ref):
        @pl.when(jnp.logical_and(pl.program_id(0) == 0, pl.program_id(1) == 0))
        def _():
            dc_ref[...] = jnp.zeros_like(dc_ref)
        dm = dm_ref[...]
        gw_ref[...] = lax.dot_general(c_ref[...], dm, (((0,), (0,)), ((), ())), precision=HI, preferred_element_type=F32)
        dc_ref[...] += lax.dot_general(dm, w_ref[...], (((1,), (1,)), ((), ())), precision=HI, preferred_element_type=F32)

    return pl.pallas_call(
        body, name="ada_bwd", grid=(depth, n // tn),
        in_specs=[pl.BlockSpec((16, d), lambda l, j: (0, 0)), pl.BlockSpec((None, d, tn), lambda l, j: (l, 0, j)),
                  pl.BlockSpec((None, 16, tn), lambda l, j: (l, 0, j))],
        out_specs=[pl.BlockSpec((None, d, tn), lambda l, j: (l, 0, j)), pl.BlockSpec((16, d), lambda l, j: (0, 0))],
        out_shape=[jax.ShapeDtypeStruct((depth, d, n), F32), jax.ShapeDtypeStruct((16, d), F32)],
        compiler_params=pltpu.CompilerParams(dimension_semantics=("arbitrary", "arbitrary")))(cond16, w_ada, dm16)


def _silu_rows(c_all, c_ctx):
    d = c_ctx.shape[-1]
    stacked = jnp.concatenate([c_all.reshape(N_DEV, d), c_ctx.reshape(1, d), jnp.zeros((16 - N_DEV - 1, d), F32)], 0)

    def body(c_ref, o_ref, ds_ref):
        v = c_ref[...]
        row = lax.broadcasted_iota(jnp.int32, v.shape, 0)
        o_ref[...] = jnp.where(row <= N_DEV, _silu(v), 0.0)
        cc = c_ref[N_DEV:N_DEV + 1, :]
        sg = jax.nn.sigmoid(cc)
        ds_ref[...] = sg * (1.0 + cc * (1.0 - sg))

    return pl.pallas_call(body, name="cond_silu", out_shape=[jax.ShapeDtypeStruct((16, d), F32), jax.ShapeDtypeStruct((1, d), F32)])(stacked)


def _flat_pack(arrs):
    flat = jnp.concatenate([a.reshape(-1) for a in arrs])
    n = flat.shape[0]
    rows = -(-n // LANE)
    rows = -(-rows // 8) * 8
    return jnp.concatenate([flat, jnp.zeros((rows * LANE - n,), F32)]).reshape(rows, LANE)


def _flat_unpack(packed, shapes):
    flat = packed.reshape(-1)
    out, o = [], 0
    for s in shapes:
        n = int(np.prod(s))
        out.append(flat[o:o + n].reshape(s))
        o += n
    return out


SMALL = ["b_ada", "ret_decay_logit", "dn_a_log", "dn_dt_bias", "dn_norm_w", "att_qn_w", "att_kn_w", "ln1_w", "ln1_b", "ln2_w", "ln2_b"]
OUT_ORDER = ['c_ctx', 'w_ada', 'b_ada', 'w_in', 'ret_decay_logit', 'dn_conv_w', 'dn_a_log', 'dn_dt_bias', 'dn_norm_w', 'att_qn_w',
             'att_kn_w', 'w_o', 'ln1_w', 'ln1_b', 'w_ffn_in', 'w_ffn_out', 'ln2_w', 'ln2_b']


def kernel(x, c, ctx, c_ctx, w_ada, b_ada, w_in, ret_decay_logit, dn_conv_w, dn_a_log, dn_dt_bias, dn_norm_w, att_qn_w, att_kn_w, w_o, ln1_w, ln1_b, w_ffn_in, w_ffn_out, ln2_w, ln2_b, loss_target, m_c_ctx, m_w_ada, m_b_ada, m_w_in, m_ret_decay_logit, m_dn_conv_w, m_dn_a_log, m_dn_dt_bias, m_dn_norm_w, m_att_qn_w, m_att_kn_w, m_w_o, m_ln1_w, m_ln1_b, m_w_ffn_in, m_w_ffn_out, m_ln2_w, m_ln2_b, v_c_ctx, v_w_ada, v_b_ada, v_w_in, v_ret_decay_logit, v_dn_conv_w, v_dn_a_log, v_dn_dt_bias, v_dn_norm_w, v_att_qn_w, v_att_kn_w, v_w_o, v_ln1_w, v_ln1_b, v_w_ffn_in, v_w_ffn_out, v_ln2_w, v_ln2_b):
    weights = dict(c_ctx=c_ctx, w_ada=w_ada, b_ada=b_ada, w_in=w_in, ret_decay_logit=ret_decay_logit, dn_conv_w=dn_conv_w,
                   dn_a_log=dn_a_log, dn_dt_bias=dn_dt_bias, dn_norm_w=dn_norm_w, att_qn_w=att_qn_w, att_kn_w=att_kn_w, w_o=w_o,
                   ln1_w=ln1_w, ln1_b=ln1_b, w_ffn_in=w_ffn_in, w_ffn_out=w_ffn_out, ln2_w=ln2_w, ln2_b=ln2_b)
    mom1 = dict(c_ctx=m_c_ctx, w_ada=m_w_ada, b_ada=m_b_ada, w_in=m_w_in, ret_decay_logit=m_ret_decay_logit, dn_conv_w=m_dn_conv_w,
                dn_a_log=m_dn_a_log, dn_dt_bias=m_dn_dt_bias, dn_norm_w=m_dn_norm_w, att_qn_w=m_att_qn_w, att_kn_w=m_att_kn_w,
                w_o=m_w_o, ln1_w=m_ln1_w, ln1_b=m_ln1_b, w_ffn_in=m_w_ffn_in, w_ffn_out=m_w_ffn_out, ln2_w=m_ln2_w, ln2_b=m_ln2_b)
    mom2 = dict(c_ctx=v_c_ctx, w_ada=v_w_ada, b_ada=v_b_ada, w_in=v_w_in, ret_decay_logit=v_ret_decay_logit, dn_conv_w=v_dn_conv_w,
                dn_a_log=v_dn_a_log, dn_dt_bias=v_dn_dt_bias, dn_norm_w=v_dn_norm_w, att_qn_w=v_att_qn_w, att_kn_w=v_att_kn_w,
                w_o=v_w_o, ln1_w=v_ln1_w, ln1_b=v_ln1_b, w_ffn_in=v_w_ffn_in, w_ffn_out=v_w_ffn_out, ln2_w=v_ln2_w, ln2_b=v_ln2_b)
    depth, d, n_ada = w_ada.shape
    me = _my_index()
    x2, ctx2, tgt2 = x[0], ctx[0], loss_target[0]

    c_all, conv_all, win_all, wo_all, wfi_all, wfo_all = exchange(
        "gather_weights", [c, dn_conv_w, w_in.astype(BF16), w_o.astype(BF16), w_ffn_in.astype(BF16), w_ffn_out.astype(BF16)],
        ['gather'] * 6)
    proj_w = w_in.shape[2] * N_DEV
    conv_full = jnp.moveaxis(conv_all, 0, 2).reshape(depth, DN_CONV_K, -1)
    conv_w8 = jnp.concatenate([conv_full, jnp.zeros((depth, 8 - DN_CONV_K, conv_full.shape[2]), F32)], 1)
    wts = dict(
        w_in=[_pad_in_proj(jnp.moveaxis(win_all[:, i], 0, 1).reshape(d, proj_w)) for i in range(depth)],
        w_o=[wo_all[:, i].reshape(d, d) for i in range(depth)],
        w_ffn_in=[wfi_all[:, i] for i in range(depth)],
        w_ffn_out=[wfo_all[:, i].reshape(-1, d) for i in range(depth)])
    likes = {k: [jnp.zeros(a.shape, F32) for a in v] for k, v in wts.items()}

    cond16, dsilu_ctx = _silu_rows(c_all, c_ctx)
    b_loc = lax.dynamic_slice_in_dim(b_ada, me * n_ada, n_ada, axis=1)[:, None, :]
    mod_loc = _ada_forward(cond16, w_ada, b_loc)
    mod_all, = exchange("gather_mods", [mod_loc], ['gather'])
    mod_full = jnp.moveaxis(mod_all, 0, 2).reshape(depth, 16, N_DEV * n_ada)
    mods = jnp.stack([mod_full[:, N_DEV], lax.dynamic_index_in_dim(mod_full, me, 1, keepdims=False)], 1).reshape(depth, 2, 6, d)

    sm = dict(ret_decay_logit=ret_decay_logit, dn_a_log=dn_a_log, dn_dt_bias=dn_dt_bias, dn_norm_w=dn_norm_w, att_qn_w=att_qn_w,
              att_kn_w=att_kn_w, ln1_w=ln1_w, ln1_b=ln1_b, ln2_w=ln2_w, ln2_b=ln2_b, conv_w8=conv_w8)
    loss, vjp = jax.vjp(lambda xx, mm, ll, ss: _forward(xx, ctx2, tgt2, mm, wts, ll, ss), x2, mods, likes, sm)
    gx, gmods, glikes, gsm = vjp(jnp.ones((), F32))
    loss = lax.psum(loss, ("x", "y", "c"))

    gm = gmods.reshape(depth, 2, 6 * d)
    small_parts = dict(gsm)
    small_parts["b_ada"] = gm[:, 0] + gm[:, 1]
    small_pack = _flat_pack([small_parts[k] for k in SMALL])
    rw, dw, aw, kw = _widths()
    g_conv = gsm["conv_w8"][:, :DN_CONV_K].reshape(depth, DN_CONV_K, N_DEV, -1)
    parts = [
        jnp.stack([jnp.moveaxis(_unpad_in_proj(g).reshape(d, N_DEV, -1), 1, 0) for g in glikes["w_in"]], 1),
        jnp.stack([g.reshape(N_DEV, -1, d) for g in glikes["w_o"]], 1),
        jnp.stack(glikes["w_ffn_in"], 1),
        jnp.stack([g.reshape(N_DEV, -1, d) for g in glikes["w_ffn_out"]], 1),
        jnp.moveaxis(g_conv, 2, 0),
    ]
    got = exchange("exchange_grads", [gm, small_pack] + parts, ['gather', 'gather'] + ['a2a'] * 5)
    gm_all, small_all = got[0], got[1]

    ctx_sum = gm_all[0, :, 0]
    for p in range(1, N_DEV):
        ctx_sum = ctx_sum + gm_all[p, :, 0]
    dm_full = jnp.concatenate([jnp.moveaxis(gm_all[:, :, 1], 0, 1), ctx_sum[:, None], jnp.zeros((depth, 16 - N_DEV - 1, 6 * d), F32)], 1)
    dm16 = lax.dynamic_slice_in_dim(dm_full, me * n_ada, n_ada, axis=2)
    g_w_ada, dcond = _ada_backward(cond16, w_ada, dm16)
    dcond_all, = exchange("gather_dcond", [dcond[N_DEV:N_DEV + 1]], ['gather'])

    out = {}

    def update(name, parts3, shape, scale=None):
        w2 = weights[name].reshape(parts3.shape[1:])
        g, dl, nm, nv = adamw("adamw_" + name, parts3, w2, mom1[name].reshape(w2.shape), mom2[name].reshape(w2.shape), scale)
        out[name] = tuple(a.reshape(shape) for a in (g, dl, nm, nv))

    update("w_in", got[2].reshape(N_DEV, depth * d, -1), w_in.shape)
    update("w_o", got[3].reshape(N_DEV, -1, d), w_o.shape)
    update("w_ffn_in", got[4].reshape(N_DEV, depth * d, -1), w_ffn_in.shape)
    update("w_ffn_out", got[5].reshape(N_DEV, -1, d), w_ffn_out.shape)
    update("dn_conv_w", got[6].reshape(N_DEV, depth * DN_CONV_K, -1), dn_conv_w.shape)
    update("w_ada", g_w_ada.reshape(1, depth * d, n_ada), w_ada.shape)
    update("c_ctx", dcond_all.reshape(N_DEV, 1, d), c_ctx.shape, scale=dsilu_ctx)

    small_shapes = [weights[k].shape for k in SMALL]
    w_pack, m_pack, v_pack = (_flat_pack([src[k] for k in SMALL]) for src in (weights, mom1, mom2))
    packs = adamw("adamw_small", small_all, w_pack, m_pack, v_pack)
    for k, vals in zip(SMALL, zip(*[_flat_unpack(pk, small_shapes) for pk in packs])):
        out[k] = vals

    res = [loss, gx[None]]
    for slot in range(4):
        res += [out[k][slot] for k in OUT_ORDER]
    return tuple(res)
```

```python
import functools
import math

import jax
import jax.numpy as jnp
import numpy as np
from jax import lax
from jax.experimental import pallas as pl
from jax.experimental.pallas import tpu as pltpu

F32 = jnp.float32
BF16 = jnp.bfloat16
HI = lax.Precision.HIGHEST

HEAD_DIM = 128
RET_HEADS = 4
DN_HEADS = 4
ATT_HEADS = 8
ATT_KV_HEADS = 2
RET_CHUNK = 128
DN_CHUNK = 64
DN_CONV_K = 5
GRID_W = 64
ROPE_THETA = 10000.0
EPS = 1e-6
ADAM_LR = 0.001
ADAM_B1 = 0.9
ADAM_B2 = 0.999
ADAM_EPS = 1e-08
ADAM_WD = 0.01
ADAM_STEP = 10
N_DEV = 8
LANE = 128
LN2 = math.log(2.0)
LOG2E = 1.0 / LN2
ROW_TILE = 256
MESH = pl.DeviceIdType.MESH


def _tile(n, cap, unit=LANE):
    best = None
    for t in range(unit, min(n, cap) + 1, unit):
        if n % t == 0:
            best = t
    return n if best is None else best


def _op(fwd, bwd):
    @jax.custom_vjp
    def op(*args):
        return fwd(*args)[0]
    op.defvjp(fwd, bwd)
    return op


def _cast(x, exact):
    return x if exact else x.astype(BF16)


def _dot(a, b, dims, exact):
    return lax.dot_general(_cast(a, exact), _cast(b, exact), (dims, ((), ())),
                           precision=HI if exact else None, preferred_element_type=F32)


def mm_nn(name, a, b, exact=False, tm_cap=1024, tn_cap=1408, tk_cap=2048):
    m, k = a.shape
    slotted = b.ndim == 3
    if slotted:
        s, _, ns = b.shape
        n = s * ns
        tn = _tile(ns, tn_cap)
        per = ns // tn
    else:
        n = b.shape[1]
        tn = _tile(n, tn_cap)
    tm = _tile(m, tm_cap, 8)
    tk = _tile(k, tk_cap)
    nk = k // tk

    def body(a_ref, b_ref, o_ref):
        kk = pl.program_id(2)

        @pl.when(kk == 0)
        def _():
            o_ref[...] = jnp.zeros_like(o_ref)
        o_ref[...] += _dot(a_ref[...], b_ref[...], ((1,), (0,)), exact)

    if slotted:
        b_spec = pl.BlockSpec((None, tk, tn), lambda i, j, kk: (j // per, kk, j % per))
    else:
        b_spec = pl.BlockSpec((tk, tn), lambda i, j, kk: (kk, j))
    return pl.pallas_call(
        body, name=name, grid=(m // tm, n // tn, nk),
        in_specs=[pl.BlockSpec((tm, tk), lambda i, j, kk: (i, kk)), b_spec],
        out_specs=pl.BlockSpec((tm, tn), lambda i, j, kk: (i, j)),
        out_shape=jax.ShapeDtypeStruct((m, n), F32),
        compiler_params=pltpu.CompilerParams(dimension_semantics=("parallel", "parallel", "arbitrary")),
    )(a, b)


def mm_nt(name, a, b, exact=False, tm_cap=1024, tn_cap=2048, tk_cap=1408):
    m, n = a.shape
    slotted = b.ndim == 3
    if slotted:
        s, k, ns = b.shape
        tk = _tile(ns, tk_cap)
        per = ns // tk
    else:
        k = b.shape[0]
        tk = _tile(n, tk_cap)
    tm = _tile(m, tm_cap, 8)
    tn = _tile(k, tn_cap)
    nk = n // tk

    def body(a_ref, b_ref, o_ref):
        kk = pl.program_id(2)

        @pl.when(kk == 0)
        def _():
            o_ref[...] = jnp.zeros_like(o_ref)
        o_ref[...] += _dot(a_ref[...], b_ref[...], ((1,), (1,)), exact)

    if slotted:
        b_spec = pl.BlockSpec((None, tn, tk), lambda i, j, kk: (kk // per, j, kk % per))
    else:
        b_spec = pl.BlockSpec((tn, tk), lambda i, j, kk: (j, kk))
    return pl.pallas_call(
        body, name=name, grid=(m // tm, k // tn, nk),
        in_specs=[pl.BlockSpec((tm, tk), lambda i, j, kk: (i, kk)), b_spec],
        out_specs=pl.BlockSpec((tm, tn), lambda i, j, kk: (i, j)),
        out_shape=jax.ShapeDtypeStruct((m, k), F32),
        compiler_params=pltpu.CompilerParams(dimension_semantics=("parallel", "parallel", "arbitrary")),
    )(a, b)


def mm_tn(name, a, b, slots=None, exact=False, tm_cap=1024, tn_cap=1408, tk_cap=1024):
    m, k = a.shape
    n = b.shape[1]
    if slots is not None:
        s, ns = slots
        tn = _tile(ns, tn_cap)
        per = ns // tn
        out_shape = jax.ShapeDtypeStruct((s, k, ns), F32)
    else:
        tn = _tile(n, tn_cap)
        out_shape = jax.ShapeDtypeStruct((k, n), F32)
    tm = _tile(k, tm_cap)
    tk = _tile(m, tk_cap, 8)
    nk = m // tk

    def body(a_ref, b_ref, o_ref):
        kk = pl.program_id(2)

        @pl.when(kk == 0)
        def _():
            o_ref[...] = jnp.zeros_like(o_ref)
        o_ref[...] += _dot(a_ref[...], b_ref[...], ((0,), (0,)), exact)

    if slots is not None:
        o_spec = pl.BlockSpec((None, tm, tn), lambda i, j, kk: (j // per, i, j % per))
    else:
        o_spec = pl.BlockSpec((tm, tn), lambda i, j, kk: (i, j))
    return pl.pallas_call(
        body, name=name, grid=(k // tm, n // tn, nk),
        in_specs=[pl.BlockSpec((tk, tm), lambda i, j, kk: (kk, i)), pl.BlockSpec((tk, tn), lambda i, j, kk: (kk, j))],
        out_specs=o_spec, out_shape=out_shape,
        compiler_params=pltpu.CompilerParams(dimension_semantics=("parallel", "parallel", "arbitrary")),
    )(a, b)


def linear(name, a, w, like):
    def fwd(a, w, like):
        return mm_nn(name + "_fwd", a, w), (a, w)

    def bwd(res, dy):
        a, w = res
        da = mm_nt(name + "_dx", dy, w)
        slots = (w.shape[0], w.shape[2]) if w.ndim == 3 else None
        dw = mm_tn(name + "_dw", a, dy, slots=slots)
        return da, jnp.zeros_like(w), dw

    return _op(fwd, bwd)(a, w, like)


def rowwise(name, f, rows, row_modes, row_diff, vecs, vec_kinds, vec_diff, out_defs, ncol=1, nctx=0):
    nr, nv = len(rows), len(vecs)
    t = rows[0].shape[0]
    tm = min(ROW_TILE, t)
    nrow = t // tm

    def row_spec(a, mode):
        w = a.shape[1]
        if mode == 'j':
            return pl.BlockSpec((tm, w // ncol), lambda i, j: (i, j))
        return pl.BlockSpec((tm, w), lambda i, j: (i, 0))

    def vec_spec(a, kind):
        nd = a.ndim
        if kind == 'shared':
            return pl.BlockSpec(a.shape, lambda i, j: (0,) * nd)
        return pl.BlockSpec((None,) + a.shape[1:], lambda i, j: ((i >= nctx).astype(jnp.int32),) + (0,) * (nd - 1))

    in_specs = [row_spec(a, m) for a, m in zip(rows, row_modes)] + [vec_spec(a, k) for a, k in zip(vecs, vec_kinds)]
    out_specs = [pl.BlockSpec((tm, w), lambda i, j: (i, j)) for w, _ in out_defs]
    out_shape = [jax.ShapeDtypeStruct((t, ncol * w), dt) for w, dt in out_defs]
    params = pltpu.CompilerParams(dimension_semantics=("arbitrary", "arbitrary"))

    def fwd_call(*args):
        def body(*refs):
            vals = [r[...] for r in refs[:nr + nv]]
            outs = f(*vals)
            for o_ref, o in zip(refs[nr + nv:], outs):
                o_ref[...] = o.astype(o_ref.dtype)
        return pl.pallas_call(body, name=name + "_fwd", grid=(nrow, ncol), in_specs=in_specs, out_specs=out_specs,
                              out_shape=out_shape, compiler_params=params)(*args)

    diff_idx = [i for i in range(nr) if row_diff[i]] + [nr + i for i in range(nv) if vec_diff[i]]
    d_rows = [i for i in range(nr) if row_diff[i]]
    d_vecs = [i for i in range(nv) if vec_diff[i]]

    def bwd_call(args, cts):
        n_in = nr + nv + len(out_defs)

        def body(*refs):
            i, j = pl.program_id(0), pl.program_id(1)
            vals = [r[...] for r in refs[:nr + nv]]
            ct = tuple(r[...] for r in refs[nr + nv:n_in])

            def g(*dvals):
                full = list(vals)
                for idx, v in zip(diff_idx, dvals):
                    full[idx] = v
                return tuple(f(*full))

            outs, vjp = jax.vjp(g, *[vals[idx] for idx in diff_idx])
            grads = vjp(tuple(c.astype(o.dtype) for c, o in zip(ct, outs)))
            out_refs = refs[n_in:]
            for p, _ in enumerate(d_rows):
                out_refs[p][...] = grads[p].astype(out_refs[p].dtype)
            for p, vi in enumerate(d_vecs):
                ref = out_refs[len(d_rows) + p]
                if vec_kinds[vi] == 'shared':
                    first = jnp.logical_and(i == 0, j == 0)
                else:
                    first = jnp.logical_and(jnp.logical_or(i == 0, i == nctx), j == 0)

                @pl.when(first)
                def _():
                    ref[...] = jnp.zeros_like(ref)
                ref[...] += grads[len(d_rows) + p].astype(F32)

        ct_specs = [pl.BlockSpec((tm, w), lambda i, j: (i, j)) for w, _ in out_defs]
        g_specs = [row_spec(rows[i], row_modes[i]) for i in d_rows] + [vec_spec(vecs[i], vec_kinds[i]) for i in d_vecs]
        g_shape = [jax.ShapeDtypeStruct(rows[i].shape, rows[i].dtype) for i in d_rows] + \
                  [jax.ShapeDtypeStruct(vecs[i].shape, F32) for i in d_vecs]
        return pl.pallas_call(body, name=name + "_bwd", grid=(nrow, ncol), in_specs=in_specs + ct_specs, out_specs=g_specs,
                              out_shape=g_shape, compiler_params=params)(*args, *cts)

    def fwd(*args):
        return tuple(fwd_call(*args)), args

    def bwd(args, cts):
        grads = bwd_call(args, cts)
        out = [None] * (nr + nv)
        for p, idx in enumerate(diff_idx):
            out[idx] = grads[p]
        for idx in range(nr + nv):
            if out[idx] is None:
                out[idx] = jnp.zeros_like(args[idx])
        return tuple(out)

    for i in d_rows:
        assert row_modes[i] == 'j' or ncol == 1
    return _op(fwd, bwd)(*rows, *vecs)


def _silu(x):
    return x * jax.nn.sigmoid(x)


def _roll_half(x):
    return pltpu.roll(x, HEAD_DIM // 2, axis=1)


@jax.custom_vjp
def _rope(x, cosf, sins):
    return x * cosf + _roll_half(x) * sins


def _rope_fwd(x, cosf, sins):
    return _rope(x, cosf, sins), (cosf, sins)


def _rope_bwd(res, dy):
    cosf, sins = res
    return dy * cosf + _roll_half(dy * sins), jnp.zeros_like(cosf), jnp.zeros_like(sins)


_rope.defvjp(_rope_fwd, _rope_bwd)


def _f_mod(x, shift, scale):
    return (x * (1.0 + scale) + shift,)


def _layer_norm(z, w, b):
    mu = jnp.mean(z, -1, keepdims=True)
    zc = z - mu
    var = jnp.mean(zc * zc, -1, keepdims=True)
    return zc * lax.rsqrt(var + EPS) * w + b


def _f_norm_mod(alpha, x, t, gate, w, b, shift, scale):
    xn = _layer_norm(alpha * x + gate * t, w, b)
    return xn, xn * (1.0 + scale) + shift


def _f_norm(alpha, x, t, gate, w, b):
    return (_layer_norm(alpha * x + gate * t, w, b),)


def _f_ret_prep(q, k, cosf, sins):
    return _rope(q, cosf, sins), _rope(k, cosf, sins) * HEAD_DIM ** -0.5


def _rms(x):
    return x * lax.rsqrt(jnp.mean(x * x, -1, keepdims=True) + EPS)


def _f_gated_out(of, ob, gate):
    return (_rms(of + ob) * _silu(gate),)


def _f_gated_out_w(of, ob, gate, w):
    return (_rms(of + ob) * w * _silu(gate),)


def _l2n(x):
    return x * lax.rsqrt(jnp.sum(x * x, -1, keepdims=True) + EPS)


def _f_dn_prep(cq, ck, cv):
    return _l2n(_silu(cq)) * HEAD_DIM ** -0.5, _l2n(_silu(ck)), _silu(cv)


def _f_qk_norm(x, cosf, sins, w):
    return (_rope(_rms(x) * w, cosf, sins),)


def _f_gates(ab, alog, dtb):
    tm = ab.shape[0]
    lane = lax.broadcasted_iota(jnp.int32, (1, LANE), 1)
    g = -jnp.exp(alog) * jax.nn.softplus(ab + dtb)
    beta = jax.nn.sigmoid(ab)
    r = lax.broadcasted_iota(jnp.int32, (tm, tm), 0)
    c = lax.broadcasted_iota(jnp.int32, (tm, tm), 1)
    same = (r // DN_CHUNK) == (c // DN_CHUNK)
    lower = jnp.where(jnp.logical_and(same, c <= r), 1.0, 0.0).astype(F32)
    upper = jnp.where(jnp.logical_and(same, c >= r), 1.0, 0.0).astype(F32)
    gl = jnp.dot(lower, g, precision=HI, preferred_element_type=F32)
    gu = jnp.dot(upper, g, precision=HI, preferred_element_type=F32)
    out = jnp.where(lane < DN_HEADS, gl, jnp.where(lane < 2 * DN_HEADS, gu, jnp.where(lane < 4 * DN_HEADS, beta, 0.0)))
    return (out,)


def swiglu(name, u, tn_cap=1408):
    t, f2 = u.shape
    ff = f2 // 2
    tn = _tile(ff, tn_cap)
    nc = ff // tn
    tm = min(ROW_TILE, t)

    def fwd_call(u):
        def body(g_ref, u_ref, o_ref):
            o_ref[...] = _silu(g_ref[...]) * u_ref[...]
        return pl.pallas_call(
            body, name=name + "_fwd", grid=(t // tm, nc),
            in_specs=[pl.BlockSpec((tm, tn), lambda i, j: (i, j)), pl.BlockSpec((tm, tn), lambda i, j: (i, j + nc))],
            out_specs=pl.BlockSpec((tm, tn), lambda i, j: (i, j)), out_shape=jax.ShapeDtypeStruct((t, ff), F32),
            compiler_params=pltpu.CompilerParams(dimension_semantics=("parallel", "parallel")))(u, u)

    def bwd_call(u, da):
        def body(own_ref, other_ref, da_ref, o_ref):
            j = pl.program_id(1)
            own, other, d = own_ref[...], other_ref[...], da_ref[...]
            sg = jax.nn.sigmoid(own)
            d_gate = d * other * (sg * (1.0 + own * (1.0 - sg)))
            d_up = d * _silu(other)
            o_ref[...] = jnp.where(j < nc, d_gate, d_up)
        return pl.pallas_call(
            body, name=name + "_bwd", grid=(t // tm, 2 * nc),
            in_specs=[pl.BlockSpec((tm, tn), lambda i, j: (i, j)), pl.BlockSpec((tm, tn), lambda i, j: (i, (j + nc) % (2 * nc))),
                      pl.BlockSpec((tm, tn), lambda i, j: (i, j % nc))],
            out_specs=pl.BlockSpec((tm, tn), lambda i, j: (i, j)), out_shape=jax.ShapeDtypeStruct((t, f2), F32),
            compiler_params=pltpu.CompilerParams(dimension_semantics=("parallel", "parallel")))(u, u, da)

    return _op(lambda u: (fwd_call(u), (u,)), lambda res, da: (bwd_call(res[0], da),))(u)


HALO = 8


def _conv_specs(t, c, tm, tc):
    nb8 = t // HALO
    per = tm // HALO
    cur = pl.BlockSpec((tm, tc), lambda j, i: (i, j))
    prev = pl.BlockSpec((HALO, tc), lambda j, i: (jnp.maximum(i * per - 1, 0), j))
    nxt = pl.BlockSpec((HALO, tc), lambda j, i: (jnp.minimum((i + 1) * per, nb8 - 1), j))
    return cur, prev, nxt


def _extended(prev_ref, cur_ref, next_ref, ext_ref, i, nrow, nctx, tm):
    has_prev = jnp.logical_and(i != 0, i != nctx)
    has_next = jnp.logical_and(i != nrow - 1, i != nctx - 1)
    ext_ref[0:HALO, :] = jnp.where(has_prev, prev_ref[...], 0.0)
    ext_ref[HALO:HALO + tm, :] = cur_ref[...]
    ext_ref[HALO + tm:, :] = jnp.where(has_next, next_ref[...], 0.0)


def _conv_call(name, x, w8, nctx, flip):
    t, c = x.shape
    tm = min(ROW_TILE, t)
    tc = _tile(c, 512)
    nrow = t // tm
    pad = DN_CONV_K // 2

    def body(cur_ref, prev_ref, next_ref, w_ref, o_ref, ext_ref):
        i = pl.program_id(1)
        _extended(prev_ref, cur_ref, next_ref, ext_ref, i, nrow, nctx, tm)
        acc = jnp.zeros((tm, tc), F32)
        for j in range(DN_CONV_K):
            wj = w_ref[(DN_CONV_K - 1 - j) if flip else j, :][None, :]
            acc = acc + wj * ext_ref[HALO - pad + j:HALO - pad + j + tm, :]
        o_ref[...] = acc

    cur, prev, nxt = _conv_specs(t, c, tm, tc)
    return pl.pallas_call(
        body, name=name, grid=(c // tc, nrow),
        in_specs=[cur, prev, nxt, pl.BlockSpec((8, tc), lambda j, i: (0, j))],
        out_specs=pl.BlockSpec((tm, tc), lambda j, i: (i, j)), out_shape=jax.ShapeDtypeStruct((t, c), F32),
        scratch_shapes=[pltpu.VMEM((tm + 2 * HALO, tc), F32)],
        compiler_params=pltpu.CompilerParams(dimension_semantics=("arbitrary", "arbitrary")))(x, x, x, w8)


def _conv_dw_call(name, x, dy, nctx):
    t, c = x.shape
    tm = min(ROW_TILE, t)
    tc = _tile(c, 512)
    nrow = t // tm
    pad = DN_CONV_K // 2

    def body(cur_ref, prev_ref, next_ref, dy_ref, o_ref, ext_ref):
        i = pl.program_id(1)
        _extended(prev_ref, cur_ref, next_ref, ext_ref, i, nrow, nctx, tm)

        @pl.when(i == 0)
        def _():
            o_ref[...] = jnp.zeros_like(o_ref)
        dy = dy_ref[...]
        rows = [jnp.sum(dy * ext_ref[HALO - pad + j:HALO - pad + j + tm, :], axis=0, keepdims=True) for j in range(DN_CONV_K)]
        rows += [jnp.zeros((1, tc), F32)] * (8 - DN_CONV_K)
        o_ref[...] += jnp.concatenate(rows, axis=0)

    cur, prev, nxt = _conv_specs(t, c, tm, tc)
    return pl.pallas_call(
        body, name=name, grid=(c // tc, nrow),
        in_specs=[cur, prev, nxt, pl.BlockSpec((tm, tc), lambda j, i: (i, j))],
        out_specs=pl.BlockSpec((8, tc), lambda j, i: (0, j)), out_shape=jax.ShapeDtypeStruct((8, c), F32),
        scratch_shapes=[pltpu.VMEM((tm + 2 * HALO, tc), F32)],
        compiler_params=pltpu.CompilerParams(dimension_semantics=("arbitrary", "arbitrary")))(x, x, x, dy)


def short_conv(name, x, w8, nctx):
    def fwd(x, w8):
        return _conv_call(name + "_fwd", x, w8, nctx, False), (x, w8)

    def bwd(res, dy):
        x, w8 = res
        return _conv_call(name + "_dx", dy, w8, nctx, True), _conv_dw_call(name + "_dw", x, dy, nctx)

    return _op(fwd, bwd)(x, w8)


def _mm_raw(a, b, form, exact):
    dims = {"nn": ((2,), (1,)), "nt": ((2,), (2,)), "tn": ((1,), (1,))}[form]
    return lax.dot_general(_cast(a, exact), _cast(b, exact), (dims, ((0,), (0,))),
                           precision=HI if exact else None, preferred_element_type=F32)


@functools.partial(jax.custom_vjp, nondiff_argnums=(2, 3))
def _mm(a, b, form, exact=False):
    return _mm_raw(a, b, form, exact)


def _mm_fwd(a, b, form, exact):
    return _mm_raw(a, b, form, exact), (a, b)


def _mm_bwd(form, exact, res, dc):
    a, b = res
    if form == "nn":
        return _mm_raw(dc, b, "nt", exact), _mm_raw(a, dc, "tn", exact)
    if form == "nt":
        return _mm_raw(dc, b, "nn", exact), _mm_raw(dc, a, "tn", exact)
    return _mm_raw(b, dc, "nt", exact), _mm_raw(a, dc, "nn", exact)


_mm.defvjp(_mm_fwd, _mm_bwd)


def chunk_scan(name, f, chunk, heads, ncc, q, k, v, gargs_f, gargs_b, pargs):
    t = q.shape[0]
    nc = t // chunk
    g_all = 2 * heads
    d = HEAD_DIM
    hw = heads * d
    ng, npar = len(gargs_f), len(pargs)

    def pos_b(n):
        return jnp.where(n < ncc, ncc - 1 - n, ncc + nc - 1 - n)

    def specs(rev_visit):
        def vis(n):
            return (nc - 1 - n) if rev_visit else n
        qf = pl.BlockSpec((chunk, hw), lambda n: (vis(n), 0))
        qb = pl.BlockSpec((chunk, hw), lambda n: (pos_b(vis(n)), 0))
        gf = [pl.BlockSpec((heads, None) + a.shape[2:], lambda n: (0, vis(n), 0, 0)) for a in gargs_f]
        gb = [pl.BlockSpec((heads, None) + a.shape[2:], lambda n: (0, pos_b(vis(n)), 0, 0)) for a in gargs_b]
        ps = [pl.BlockSpec(a.shape, lambda n: (0, 0, 0)) for a in pargs]
        ss = pl.BlockSpec((g_all, None, d, d), lambda n: (0, vis(n), 0, 0))
        return qf, qb, gf, gb, ps, ss

    params = pltpu.CompilerParams(dimension_semantics=("arbitrary",))
    n_in = 6 + 2 * ng + npar

    def cols(h):
        return slice(h * d, (h + 1) * d)

    def stacked_inputs(refs):
        qkv = [jnp.stack([refs[i][:, cols(h)] for h in range(heads)] + [refs[3 + i][:, cols(h)] for h in range(heads)])
               for i in range(3)]
        gar = [jnp.concatenate([refs[6 + i][...], refs[6 + ng + i][...]], axis=0) for i in range(ng)]
        par = [r[...] for r in refs[6 + 2 * ng:n_in]]
        return qkv + gar + par

    def scatter(vals, refs_f, refs_b):
        for r_f, r_b, val in zip(refs_f, refs_b, vals):
            for h in range(heads):
                r_f[:, cols(h)] = val[h]
                r_b[:, cols(h)] = val[heads + h]

    def fwd_call(q, k, v, *rest):
        qf, qb, gf, gb, ps, ss = specs(False)

        def body(*refs):
            of_ref, ob_ref, sp_ref, s_ref = refs[n_in:]

            @pl.when(pl.program_id(0) == 0)
            def _():
                s_ref[...] = jnp.zeros_like(s_ref)
            s_prev = s_ref[...]
            sp_ref[...] = s_prev
            o, s_new = f(heads, *stacked_inputs(refs), s_prev)
            scatter([o], [of_ref], [ob_ref])
            s_ref[...] = s_new

        oshape = jax.ShapeDtypeStruct((t, hw), F32)
        return pl.pallas_call(
            body, name=name + "_fwd", grid=(nc,), in_specs=[qf] * 3 + [qb] * 3 + gf + gb + ps, out_specs=[qf, qb, ss],
            out_shape=[oshape, oshape, jax.ShapeDtypeStruct((g_all, nc, d, d), F32)],
            scratch_shapes=[pltpu.VMEM((g_all, d, d), F32)], compiler_params=params)(q, k, v, q, k, v, *rest)

    def bwd_call(q, k, v, rest, s_prev, do_f, do_b):
        qf, qb, gf, gb, ps, ss = specs(True)

        def body(*refs):
            sp_ref, dof_ref, dob_ref = refs[n_in:n_in + 3]
            outs = refs[n_in + 3:]
            dqkv_f, dqkv_b = outs[0:3], outs[3:6]
            dg_f, dg_b = outs[6:6 + ng], outs[6 + ng:6 + 2 * ng]
            dp_refs = outs[6 + 2 * ng:6 + 2 * ng + npar]
            ds_ref = outs[6 + 2 * ng + npar]

            @pl.when(pl.program_id(0) == 0)
            def _():
                ds_ref[...] = jnp.zeros_like(ds_ref)
                for r in dp_refs:
                    r[...] = jnp.zeros_like(r)
            _, vjp = jax.vjp(functools.partial(f, heads), *stacked_inputs(refs), sp_ref[...])
            do = jnp.stack([dof_ref[:, cols(h)] for h in range(heads)] + [dob_ref[:, cols(h)] for h in range(heads)])
            grads = vjp((do, ds_ref[...]))
            scatter(grads[:3], dqkv_f, dqkv_b)
            for r_f, r_b, gr in zip(dg_f, dg_b, grads[3:3 + ng]):
                r_f[...] = gr[:heads]
                r_b[...] = gr[heads:]
            for r, gr in zip(dp_refs, grads[3 + ng:3 + ng + npar]):
                r[...] += gr
            ds_ref[...] = grads[3 + ng + npar]

        dshape = jax.ShapeDtypeStruct((t, hw), F32)
        return pl.pallas_call(
            body, name=name + "_bwd", grid=(nc,), in_specs=[qf] * 3 + [qb] * 3 + gf + gb + ps + [ss, qf, qb],
            out_specs=[qf] * 3 + [qb] * 3 + gf + gb + ps,
            out_shape=[dshape] * 6 + [jax.ShapeDtypeStruct(a.shape, F32) for a in list(gargs_f) + list(gargs_b) + list(pargs)],
            scratch_shapes=[pltpu.VMEM((g_all, d, d), F32)], compiler_params=params)(q, k, v, q, k, v, *rest, s_prev, do_f, do_b)

    def fwd(q, k, v, *rest):
        o_f, o_b, s_prev = fwd_call(q, k, v, *rest)
        return (o_f, o_b), (q, k, v, rest, s_prev)

    def bwd(res, do):
        q, k, v, rest, s_prev = res
        grads = bwd_call(q, k, v, rest, s_prev, do[0], do[1])
        return tuple(grads[i] + grads[3 + i] for i in range(3)) + tuple(grads[6:])

    return _op(fwd, bwd)(q, k, v, *gargs_f, *gargs_b, *pargs)


def _order_masks(heads, c):
    g = 2 * heads
    rev = lax.broadcasted_iota(jnp.int32, (g, c, c), 0) >= heads
    i = lax.broadcasted_iota(jnp.int32, (g, c, c), 1)
    j = lax.broadcasted_iota(jnp.int32, (g, c, c), 2)
    rel = jnp.where(rev, j - i, i - j)
    pos = lax.broadcasted_iota(jnp.int32, (g, c, 1), 1)
    p = jnp.where(lax.broadcasted_iota(jnp.int32, (g, c, 1), 0) >= heads, c - 1 - pos, pos)
    return rel, p


def _f_retention(heads, q, k, v, logit, s):
    c = RET_CHUNK
    lg = jax.nn.log_sigmoid(logit)
    rel, p = _order_masks(heads, c)
    rel = rel.astype(F32)
    p = p.astype(F32)
    decay = jnp.where(rel >= 0, jnp.exp(jnp.maximum(rel, 0.0) * lg), 0.0)
    o_intra = _mm(_mm(q, k, "nt") * decay, v, "nn")
    q_decay = jnp.exp((p + 1.0) * lg)
    k_decay = jnp.exp((c - 1.0 - p) * lg)
    o_inter = _mm(q * q_decay, s, "nn")
    s_new = s * jnp.exp(c * lg) + _mm(k * k_decay, v, "tn")
    return o_intra + o_inter, s_new


def _f_delta(heads, q, k, v, gc, gr, bc, s):
    c = DN_CHUNK
    rel, p = _order_masks(heads, c)
    tri = rel >= 0
    strict = rel > 0
    decay = jnp.where(tri, jnp.exp(jnp.where(tri, gc - gr, 0.0)), 0.0)
    kb = k * bc
    vb = v * bc
    a = jnp.where(strict, _mm(kb, k, "nt") * decay, 0.0)
    pw = -a
    tinv = jnp.where(rel == 0, 1.0, 0.0).astype(F32) + pw
    for _ in range(int(math.log2(c)) - 1):
        pw = _mm(pw, pw, "nn", True)
        tinv = tinv + _mm(tinv, pw, "nn", True)
    eg = jnp.exp(gc)
    w_val = _mm(tinv, vb, "nn")
    k_cum = _mm(tinv, kb * eg, "nn")
    qk = _mm(q, k, "nt") * decay
    g_last = jnp.sum(jnp.where(p == c - 1, gc, 0.0), axis=1, keepdims=True)
    k_g = k * jnp.exp(g_last - gc)
    v_new = w_val - _mm(k_cum, s, "nn")
    o = _mm(q * eg, s, "nn") + _mm(qk, v_new, "nn")
    s_new = s * jnp.exp(g_last) + _mm(k_g, v_new, "tn")
    return o, s_new


def _att_tiles(tq, tk):
    return _tile(tq, 1024, 8), _tile(tk, 1408, 8)


def _grid_ends(grid):
    ids = [pl.program_id(a) for a in range(len(grid))]
    first = functools.reduce(jnp.logical_and, [i == 0 for i in ids])
    last = functools.reduce(jnp.logical_and, [i == n - 1 for i, n in zip(ids, grid)])
    return first, last


def _att_fwd_call(name, q, k, v, rider):
    tq, tk = q.shape[0], k.shape[0]
    h, kvh = q.shape[1] // HEAD_DIM, k.shape[1] // HEAD_DIM
    grp = h // kvh
    bq, bk = _att_tiles(tq, tk)
    nk = tk // bk
    scale = HEAD_DIM ** -0.5
    grid = (h, tq // bq, nk)
    nx = rider.n

    def body(*refs):
        q_ref, k_ref, v_ref = refs[:3]
        o_ref, lse_ref = refs[3 + nx:5 + nx]
        m_sc, l_sc, acc_sc, q2_sc = refs[5 + 2 * nx:9 + 2 * nx]
        rider.run(refs[3:3 + nx], refs[5 + nx:5 + 2 * nx], refs[9 + 2 * nx:], *_grid_ends(grid))
        kj = pl.program_id(2)

        @pl.when(kj == 0)
        def _():
            m_sc[...] = jnp.full_like(m_sc, -jnp.inf)
            l_sc[...] = jnp.zeros_like(l_sc)
            acc_sc[...] = jnp.zeros_like(acc_sc)
            q2_sc[...] = (q_ref[...] * (scale * LOG2E)).astype(q2_sc.dtype)
        s = _dot(q2_sc[...], k_ref[...], ((1,), (1,)), False)
        m_prev = m_sc[...]
        m_new = jnp.maximum(m_prev, jnp.max(s, axis=1, keepdims=True))
        p = jnp.exp2(s - m_new)
        alpha = jnp.exp2(m_prev - m_new)
        l_sc[...] = alpha * l_sc[...] + jnp.sum(p, axis=1, keepdims=True)
        acc_sc[...] = alpha * acc_sc[...] + _dot(p, v_ref[...], ((1,), (0,)), False)
        m_sc[...] = m_new

        @pl.when(kj == nk - 1)
        def _():
            o_ref[...] = acc_sc[...] / l_sc[...]
            lse_ref[...] = m_sc[...] * LN2 + jnp.log(l_sc[...])

    res = pl.pallas_call(
        body, name=name, grid=grid,
        in_specs=[pl.BlockSpec((bq, HEAD_DIM), lambda hh, i, j: (i, hh)),
                  pl.BlockSpec((bk, HEAD_DIM), lambda hh, i, j: (j, hh // grp)),
                  pl.BlockSpec((bk, HEAD_DIM), lambda hh, i, j: (j, hh // grp))] + rider.in_specs,
        out_specs=[pl.BlockSpec((bq, HEAD_DIM), lambda hh, i, j: (i, hh)),
                   pl.BlockSpec((None, bq, 1), lambda hh, i, j: (hh, i, 0))] + rider.out_specs,
        out_shape=[jax.ShapeDtypeStruct((tq, h * HEAD_DIM), F32), jax.ShapeDtypeStruct((h, tq, 1), F32)] + rider.out_shape,
        scratch_shapes=[pltpu.VMEM((bq, 1), F32), pltpu.VMEM((bq, 1), F32), pltpu.VMEM((bq, HEAD_DIM), F32),
                        pltpu.VMEM((bq, HEAD_DIM), BF16)] + rider.scratch,
        compiler_params=pltpu.CompilerParams(dimension_semantics=("arbitrary",) * 3))(q, k, v, *rider.arrays)
    return res[0], res[1], list(res[2:])


def _att_dq_call(name, q, k, v, o, lse, do, rider):
    tq, tk = q.shape[0], k.shape[0]
    h, kvh = q.shape[1] // HEAD_DIM, k.shape[1] // HEAD_DIM
    grp = h // kvh
    bq, bk = _att_tiles(tq, tk)
    nk = tk // bk
    scale = HEAD_DIM ** -0.5
    grid = (h, tq // bq, nk)
    nx = rider.n

    def body(*refs):
        q_ref, k_ref, v_ref, o_ref, lse_ref, do_ref = refs[:6]
        dq_ref = refs[6 + nx]
        delta_sc, lse2_sc, q2_sc = refs[7 + 2 * nx:10 + 2 * nx]
        rider.run(refs[6:6 + nx], refs[7 + nx:7 + 2 * nx], refs[10 + 2 * nx:], *_grid_ends(grid))
        kj = pl.program_id(2)

        @pl.when(kj == 0)
        def _():
            dq_ref[...] = jnp.zeros_like(dq_ref)
            delta_sc[...] = jnp.sum(do_ref[...] * o_ref[...], axis=1, keepdims=True)
            lse2_sc[...] = lse_ref[...] * LOG2E
            q2_sc[...] = (q_ref[...] * (scale * LOG2E)).astype(q2_sc.dtype)
        s = _dot(q2_sc[...], k_ref[...], ((1,), (1,)), False)
        p = jnp.exp2(s - lse2_sc[...])
        dp = _dot(do_ref[...], v_ref[...], ((1,), (1,)), False)
        ds = p * (dp - delta_sc[...])
        dq_ref[...] += _dot(ds, k_ref[...], ((1,), (0,)), False) * scale

    qspec = pl.BlockSpec((bq, HEAD_DIM), lambda hh, i, j: (i, hh))
    kspec = pl.BlockSpec((bk, HEAD_DIM), lambda hh, i, j: (j, hh // grp))
    res = pl.pallas_call(
        body, name=name, grid=grid,
        in_specs=[qspec, kspec, kspec, qspec, pl.BlockSpec((None, bq, 1), lambda hh, i, j: (hh, i, 0)), qspec] + rider.in_specs,
        out_specs=[qspec] + rider.out_specs, out_shape=[jax.ShapeDtypeStruct(q.shape, F32)] + rider.out_shape,
        scratch_shapes=[pltpu.VMEM((bq, 1), F32), pltpu.VMEM((bq, 1), F32), pltpu.VMEM((bq, HEAD_DIM), BF16)] + rider.scratch,
        compiler_params=pltpu.CompilerParams(dimension_semantics=("arbitrary",) * 3))(q, k, v, o, lse, do, *rider.arrays)
    return res[0], list(res[1:])


def _att_dkv_call(name, q, k, v, o, lse, do, rider):
    tq, tk = q.shape[0], k.shape[0]
    h, kvh = q.shape[1] // HEAD_DIM, k.shape[1] // HEAD_DIM
    grp = h // kvh
    bq, bk = _att_tiles(tq, tk)
    nq = tq // bq
    nr = grp * nq
    scale = HEAD_DIM ** -0.5
    grid = (kvh, tk // bk, nr)
    nx = rider.n

    def body(*refs):
        q_ref, k_ref, v_ref, o_ref, lse_ref, do_ref = refs[:6]
        dk_ref, dv_ref = refs[6 + nx:8 + nx]
        rider.run(refs[6:6 + nx], refs[8 + nx:8 + 2 * nx], refs[8 + 2 * nx:], *_grid_ends(grid))
        r = pl.program_id(2)

        @pl.when(r == 0)
        def _():
            dk_ref[...] = jnp.zeros_like(dk_ref)
            dv_ref[...] = jnp.zeros_like(dv_ref)
        q2 = _cast(q_ref[...] * (scale * LOG2E), False)
        s = _dot(q2, k_ref[...], ((1,), (1,)), False)
        p = jnp.exp2(s - lse_ref[...] * LOG2E)
        dv_ref[...] += _dot(p, do_ref[...], ((0,), (0,)), False)
        dp = _dot(do_ref[...], v_ref[...], ((1,), (1,)), False)
        delta = jnp.sum(do_ref[...] * o_ref[...], axis=1, keepdims=True)
        ds = p * (dp - delta)
        dk_ref[...] += _dot(ds, q2, ((0,), (0,)), False) * LN2

    qspec = pl.BlockSpec((bq, HEAD_DIM), lambda kh, j, r: (r % nq, kh * grp + r // nq))
    kspec = pl.BlockSpec((bk, HEAD_DIM), lambda kh, j, r: (j, kh))
    res = pl.pallas_call(
        body, name=name, grid=grid,
        in_specs=[qspec, kspec, kspec, qspec, pl.BlockSpec((None, bq, 1), lambda kh, j, r: (kh * grp + r // nq, r % nq, 0)), qspec]
                 + rider.in_specs,
        out_specs=[kspec, kspec] + rider.out_specs,
        out_shape=[jax.ShapeDtypeStruct(k.shape, F32), jax.ShapeDtypeStruct(v.shape, F32)] + rider.out_shape,
        scratch_shapes=rider.scratch,
        compiler_params=pltpu.CompilerParams(dimension_semantics=("arbitrary",) * 3))(q, k, v, o, lse, do, *rider.arrays)
    return res[0], res[1], list(res[2:])


def attention(name, q, k, v, shards=(), carry=()):
    shards, carry = list(shards), list(carry)
    half = (len(carry) + 1) // 2

    def fwd(q, k, v, shards, carry):
        o, lse, gathered = _att_fwd_call(name + "_fwd", q, k, v, _Rider(shards, 'gather'))
        return (o, gathered, carry), (q, k, v, o, lse, shards)

    def bwd(res, cts):
        q, k, v, o, lse, shards = res
        do, _, payload = cts
        dq, got_a = _att_dq_call(name + "_dq", q, k, v, o, lse, do, _Rider(payload[:half], 'a2a'))
        dk, dv, got_b = _att_dkv_call(name + "_dkv", q, k, v, o, lse, do, _Rider(payload[half:], 'a2a'))
        return dq, dk, dv, [jnp.zeros_like(a) for a in shards], got_a + got_b

    return _op(fwd, bwd)(q, k, v, shards, carry)


def loss_head(y, target):
    t, d = y.shape
    tm = min(ROW_TILE, t)

    def fwd(y, target):
        def body(y_ref, t_ref, l_ref, dy_ref):
            @pl.when(pl.program_id(0) == 0)
            def _():
                l_ref[...] = jnp.zeros_like(l_ref)
            err = y_ref[...] - t_ref[...]
            row = jnp.sum(err * err, axis=1, keepdims=True) * (1.0 / d)
            l_ref[...] += 0.5 * jnp.sum(row, axis=0, keepdims=True)
            dy_ref[...] = err * (1.0 / d)
        loss, dy = pl.pallas_call(
            body, name="loss_head", grid=(t // tm,),
            in_specs=[pl.BlockSpec((tm, d), lambda i: (i, 0)), pl.BlockSpec((tm, d), lambda i: (i, 0))],
            out_specs=[pl.BlockSpec((1, 1), lambda i: (0, 0)), pl.BlockSpec((tm, d), lambda i: (i, 0))],
            out_shape=[jax.ShapeDtypeStruct((1, 1), F32), jax.ShapeDtypeStruct((t, d), F32)],
            compiler_params=pltpu.CompilerParams(dimension_semantics=("arbitrary",)))(y, target)
        return loss[0, 0], (dy,)

    def bwd(res, g):
        return res[0] * g, jnp.zeros_like(res[0])

    return _op(fwd, bwd)(y, target)


def _my_index():
    return 4 * lax.axis_index("x") + 2 * lax.axis_index("y") + lax.axis_index("c")


def _xch_copies(ins, outs, modes, send_sems, recv_sems, local_sems):
    x, y, c = lax.axis_index("x"), lax.axis_index("y"), lax.axis_index("c")
    me = 4 * x + 2 * y + c
    copies = []
    for a in range(len(ins)):
        gather = modes[a] == 'gather'
        copies.append(pltpu.make_async_copy(ins[a] if gather else ins[a].at[me], outs[a].at[me], local_sems.at[a]))
        for dist in range(1, N_DEV):
            px = (1 - x) if dist & 4 else x
            py = (1 - y) if dist & 2 else y
            pc = (1 - c) if dist & 1 else c
            peer = 4 * px + 2 * py + pc
            copies.append(pltpu.make_async_remote_copy(
                src_ref=ins[a] if gather else ins[a].at[peer], dst_ref=outs[a].at[me],
                send_sem=send_sems.at[a * (N_DEV - 1) + dist - 1], recv_sem=recv_sems.at[a * (N_DEV - 1) + dist - 1],
                device_id=(px, py, pc), device_id_type=MESH))
    return copies


def _xch_out_shapes(arrays, modes):
    return [jax.ShapeDtypeStruct(((N_DEV,) + a.shape) if m == 'gather' else a.shape, a.dtype) for a, m in zip(arrays, modes)]


def _xch_sems(n):
    return [pltpu.SemaphoreType.DMA((n * (N_DEV - 1),)), pltpu.SemaphoreType.DMA((n * (N_DEV - 1),)), pltpu.SemaphoreType.DMA((n,))]


def exchange(name, arrays, modes):
    n = len(arrays)

    def body(*refs):
        copies = _xch_copies(refs[:n], refs[n:2 * n], modes, *refs[2 * n:])
        for cp in copies:
            cp.start()
        for cp in copies:
            cp.wait()

    hbm = pl.BlockSpec(memory_space=pl.ANY)
    return pl.pallas_call(
        body, name=name, in_specs=[hbm] * n, out_specs=[hbm] * n, out_shape=_xch_out_shapes(arrays, modes),
        scratch_shapes=_xch_sems(n), compiler_params=pltpu.CompilerParams(has_side_effects=True))(*arrays)


class _Rider:
    def __init__(self, arrays, mode):
        self.arrays, self.n, self.modes = list(arrays), len(arrays), [mode] * len(arrays)
        self.in_specs = [pl.BlockSpec(memory_space=pl.ANY)] * self.n
        self.out_specs = list(self.in_specs)
        self.out_shape = _xch_out_shapes(self.arrays, self.modes)
        self.scratch = _xch_sems(self.n) if self.n else []

    def run(self, ins, outs, sems, first, last):
        if not self.n:
            return

        @pl.when(first)
        def _():
            for cp in _xch_copies(ins, outs, self.modes, *sems):
                cp.start()

        @pl.when(last)
        def _():
            for cp in _xch_copies(ins, outs, self.modes, *sems):
                cp.wait()


def _adamw_math(w, g, m, v):
    m = ADAM_B1 * m + (1.0 - ADAM_B1) * g
    v = ADAM_B2 * v + (1.0 - ADAM_B2) * (g * g)
    m_hat = m / (1.0 - ADAM_B1 ** ADAM_STEP)
    v_hat = v / (1.0 - ADAM_B2 ** ADAM_STEP)
    delta = -ADAM_LR * (m_hat / (jnp.sqrt(v_hat) + ADAM_EPS) + ADAM_WD * w)
    return delta, m, v


def adamw(name, parts, w, m, v, scale=None, row_bytes_cap=1 << 20):
    p, r, c = parts.shape
    tr = _tile(r, max(8, row_bytes_cap // (4 * c) // 8 * 8), 8)
    extra = [] if scale is None else [scale]

    def body(p_ref, *refs):
        w_ref, m_ref, v_ref, g_ref, d_ref, nm_ref, nv_ref = refs[len(extra):]
        g = p_ref[0]
        for q in range(1, p):
            g = g + p_ref[q]
        if extra:
            g = g * refs[0][...]
        delta, nm, nv = _adamw_math(w_ref[...], g, m_ref[...], v_ref[...])
        g_ref[...] = g
        d_ref[...] = delta
        nm_ref[...] = nm
        nv_ref[...] = nv

    spec = pl.BlockSpec((tr, c), lambda i: (i, 0))
    shape = jax.ShapeDtypeStruct((r, c), F32)
    return pl.pallas_call(
        body, name=name, grid=(r // tr,),
        in_specs=[pl.BlockSpec((p, tr, c), lambda i: (0, i, 0))] + [spec] * (3 + len(extra)),
        out_specs=[spec] * 4, out_shape=[shape] * 4,
        compiler_params=pltpu.CompilerParams(dimension_semantics=("parallel",)))(parts, *extra, w, m, v)


def _rope_tables(n_lat, n_ctx):
    rows = n_lat // GRID_W
    row = jnp.repeat(jnp.arange(rows, dtype=F32), GRID_W)
    col = jnp.tile(jnp.arange(GRID_W, dtype=F32), rows)
    n_freq = HEAD_DIM // 4
    inv = ROPE_THETA ** (-jnp.arange(n_freq, dtype=F32) / n_freq)
    ang = jnp.concatenate([row[:, None] * inv, col[:, None] * inv], -1)
    cos, sin = jnp.cos(ang), jnp.sin(ang)
    cosf = jnp.concatenate([jnp.ones((n_ctx, HEAD_DIM), F32), jnp.concatenate([cos, cos], -1)], 0)
    sins = jnp.concatenate([jnp.zeros((n_ctx, HEAD_DIM), F32), jnp.concatenate([-sin, sin], -1)], 0)
    return cosf, sins


def _widths():
    rw, dw, aw, kw = RET_HEADS * HEAD_DIM, DN_HEADS * HEAD_DIM, ATT_HEADS * HEAD_DIM, ATT_KV_HEADS * HEAD_DIM
    return rw, dw, aw, kw


def _pad_in_proj(full):
    rw, dw, aw, kw = _widths()
    n_ab = 4 * DN_HEADS
    a0 = 4 * rw + 4 * dw
    main = jnp.concatenate([full[:, :a0], full[:, a0 + n_ab:]], 1)
    ab = jnp.concatenate([full[:, a0:a0 + n_ab], jnp.zeros((full.shape[0], LANE - n_ab), full.dtype)], 1)
    return jnp.concatenate([main, ab], 1)


def _unpad_in_proj(dpad):
    rw, dw, aw, kw = _widths()
    n_ab = 4 * DN_HEADS
    a0 = 4 * rw + 4 * dw
    main_w = dpad.shape[1] - LANE
    return jnp.concatenate([dpad[:, :a0], dpad[:, main_w:main_w + n_ab], dpad[:, a0:main_w]], 1)


def _chunked(a, chunk):
    t, g = a.shape
    at = a.T.reshape(g, t // chunk, chunk)
    return at[..., None], at[:, :, None, :]


def _mixer(h, wpad, like_in, sm, cosf, sins, n_ctx, tag, shards, carry):
    rw, dw, aw, kw = _widths()
    t = h.shape[0]
    nctx = n_ctx // min(ROW_TILE, t)
    p = linear(f"in_proj{tag}", h, wpad, like_in)
    o = 0
    rq, rk, rv, rg = (p[:, o + n * rw:o + (n + 1) * rw] for n in range(4))
    o += 4 * rw
    dqkv, dz = p[:, o:o + 3 * dw], p[:, o + 3 * dw:o + 4 * dw]
    o += 4 * dw
    aq, ak, av = p[:, o:o + aw], p[:, o + aw:o + aw + kw], p[:, o + aw + kw:o + aw + 2 * kw]
    ab = p[:, o + aw + 2 * kw:]

    qr, kr = rowwise(f"ret_prep{tag}", _f_ret_prep, [rq, rk, cosf, sins], ['j', 'j', 'b', 'b'], [True, True, False, False],
                     [], [], [], [(HEAD_DIM, F32)] * 2, ncol=RET_HEADS, nctx=nctx)
    logit = sm["ret_decay_logit"].reshape(2 * RET_HEADS, 1, 1)
    o_ret = chunk_scan(f"ret_scan{tag}", _f_retention, RET_CHUNK, RET_HEADS, n_ctx // RET_CHUNK, qr, kr, rv, [], [], [logit])
    y_ret, = rowwise(f"ret_out{tag}", _f_gated_out, [o_ret[0], o_ret[1], rg], ['j'] * 3, [True] * 3, [], [], [],
                     [(HEAD_DIM, F32)], ncol=RET_HEADS, nctx=nctx)

    conv = short_conv(f"dn_conv{tag}", dqkv, sm["conv_w8"], nctx)
    dq, dk, dv = rowwise(f"dn_prep{tag}", _f_dn_prep, [conv[:, :dw], conv[:, dw:2 * dw], conv[:, 2 * dw:]], ['j'] * 3, [True] * 3,
                         [], [], [], [(HEAD_DIM, F32)] * 3, ncol=DN_HEADS, nctx=nctx)
    pad8 = lambda a: jnp.concatenate([a.reshape(1, 2 * DN_HEADS), jnp.zeros((1, LANE - 2 * DN_HEADS), F32)], 1)
    gb, = rowwise(f"dn_gates{tag}", _f_gates, [ab], ['j'], [True], [pad8(sm["dn_a_log"]), pad8(sm["dn_dt_bias"])],
                  ['shared'] * 2, [True] * 2, [(LANE, F32)], nctx=nctx)
    g_col, g_row = _chunked(gb[:, :2 * DN_HEADS], DN_CHUNK)
    b_col, _ = _chunked(gb[:, 2 * DN_HEADS:4 * DN_HEADS], DN_CHUNK)
    o_dn = chunk_scan(f"dn_scan{tag}", _f_delta, DN_CHUNK, DN_HEADS, n_ctx // DN_CHUNK, dq, dk, dv,
                      [a[:DN_HEADS] for a in (g_col, g_row, b_col)], [a[DN_HEADS:] for a in (g_col, g_row, b_col)], [])
    y_dn, = rowwise(f"dn_out{tag}", _f_gated_out_w, [o_dn[0], o_dn[1], dz], ['j'] * 3, [True] * 3,
                    [sm["dn_norm_w"].reshape(1, HEAD_DIM)], ['shared'], [True], [(HEAD_DIM, F32)], ncol=DN_HEADS, nctx=nctx)

    qn, = rowwise(f"att_qn{tag}", _f_qk_norm, [aq, cosf, sins], ['j', 'b', 'b'], [True, False, False],
                  [sm["att_qn_w"].reshape(1, HEAD_DIM)], ['shared'], [True], [(HEAD_DIM, F32)], ncol=ATT_HEADS, nctx=nctx)
    kn, = rowwise(f"att_kn{tag}", _f_qk_norm, [ak, cosf, sins], ['j', 'b', 'b'], [True, False, False],
                  [sm["att_kn_w"].reshape(1, HEAD_DIM)], ['shared'], [True], [(HEAD_DIM, F32)], ncol=ATT_KV_HEADS, nctx=nctx)
    y_att_lat, gathered, carry = attention(f"att_lat{tag}", qn[n_ctx:], kn, av, shards, carry)
    return y_ret, y_dn, y_att_lat, (qn, kn, av), gathered, carry


def _layer(i, depth, n_ctx, cosf, sins, wt, shards, xs, mod, like, sm, carry):
    alpha = (2 * depth) ** 0.25
    d = xs.shape[1]
    last = i == depth - 1
    tag = f"_{i}"
    nctx = n_ctx // min(ROW_TILE, xs.shape[0])
    seg = lambda kk: mod[:, kk][:, None, :]
    lat = lambda kk: mod[1, kk][None, :]
    h, = rowwise(f"mod1{tag}", _f_mod, [xs], ['j'], [True], [seg(0), seg(1)], ['seg'] * 2, [True] * 2, [(d, F32)], nctx=nctx)
    y_ret, y_dn, y_att_lat, (qn, kn, av), gathered, carry = _mixer(h, wt["w_in"], like["w_in"], sm, cosf, sins, n_ctx, tag,
                                                                  shards, carry)
    vec = lambda a: a.reshape(1, d)
    ln1 = [vec(sm["ln1_w"]), vec(sm["ln1_b"])]
    ln2 = [vec(sm["ln2_w"]), vec(sm["ln2_b"])]
    if last:
        y = jnp.concatenate([y_ret[n_ctx:], y_dn[n_ctx:], y_att_lat], 1)
        xs = xs[n_ctx:]
        nctx = 0
        kinds = ['shared']
        g1, sh2, sc2, g2 = lat(2), lat(3), lat(4), lat(5)
    else:
        y_att_ctx, _, _ = attention(f"att_ctx{tag}", qn[:n_ctx], kn[:n_ctx], av[:n_ctx])
        y = jnp.concatenate([y_ret, y_dn, jnp.concatenate([y_att_ctx, y_att_lat], 0)], 1)
        kinds = ['seg']
        g1, sh2, sc2, g2 = seg(2), seg(3), seg(4), seg(5)
    tt = linear(f"out_proj{tag}", y, wt["w_o"], like["w_o"])
    x1, h2 = rowwise(f"norm1{tag}", functools.partial(_f_norm_mod, alpha), [xs, tt], ['j', 'j'], [True, True],
                     [g1] + ln1 + [sh2, sc2], kinds + ['shared'] * 2 + kinds * 2, [True] * 5, [(d, F32)] * 2, nctx=nctx)
    u = linear(f"ffn_in{tag}", h2, wt["w_ffn_in"], like["w_ffn_in"])
    act = swiglu(f"swiglu{tag}", u)
    t2 = linear(f"ffn_out{tag}", act, wt["w_ffn_out"], like["w_ffn_out"])
    xs, = rowwise(f"norm2{tag}", functools.partial(_f_norm, alpha), [x1, t2], ['j', 'j'], [True, True],
                  [g2] + ln2, kinds + ['shared'] * 2, [True] * 3, [(d, F32)], nctx=nctx)
    return (xs, carry), gathered


def _ada_forward(cond16, w_ada, b_loc):
    depth, d, n = w_ada.shape
    tn = _tile(n, 512)

    def body(c_ref, w_ref, b_ref, o_ref):
        o_ref[...] = jnp.dot(c_ref[...], w_ref[...], precision=HI, preferred_element_type=F32) + b_ref[...]

    return pl.pallas_call(
        body, name="ada_fwd", grid=(depth, n // tn),
        in_specs=[pl.BlockSpec((16, d), lambda l, j: (0, 0)), pl.BlockSpec((None, d, tn), lambda l, j: (l, 0, j)),
                  pl.BlockSpec((None, 1, tn), lambda l, j: (l, 0, j))],
        out_specs=pl.BlockSpec((None, 16, tn), lambda l, j: (l, 0, j)), out_shape=jax.ShapeDtypeStruct((depth, 16, n), F32),
        compiler_params=pltpu.CompilerParams(dimension_semantics=("parallel", "parallel")))(cond16, w_ada, b_loc)


def _ada_backward(cond16, w_ada, dm16):
    depth, d, n = w_ada.shape
    tn = _tile(n, 512)

    def body(c_ref, w_ref, dm_ref, gw_ref, dc_ref):
        @pl.when(jnp.logical_and(pl.program_id(0) == 0, pl.program_id(1) == 0))
        def _():
            dc_ref[...] = jnp.zeros_like(dc_ref)
        dm = dm_ref[...]
        gw_ref[...] = lax.dot_general(c_ref[...], dm, (((0,), (0,)), ((), ())), precision=HI, preferred_element_type=F32)
        dc_ref[...] += lax.dot_general(dm, w_ref[...], (((1,), (1,)), ((), ())), precision=HI, preferred_element_type=F32)

    return pl.pallas_call(
        body, name="ada_bwd", grid=(depth, n // tn),
        in_specs=[pl.BlockSpec((16, d), lambda l, j: (0, 0)), pl.BlockSpec((None, d, tn), lambda l, j: (l, 0, j)),
                  pl.BlockSpec((None, 16, tn), lambda l, j: (l, 0, j))],
        out_specs=[pl.BlockSpec((None, d, tn), lambda l, j: (l, 0, j)), pl.BlockSpec((16, d), lambda l, j: (0, 0))],
        out_shape=[jax.ShapeDtypeStruct((depth, d, n), F32), jax.ShapeDtypeStruct((16, d), F32)],
        compiler_params=pltpu.CompilerParams(dimension_semantics=("arbitrary", "arbitrary")))(cond16, w_ada, dm16)


def _silu_rows(c_all, c_ctx):
    d = c_ctx.shape[-1]
    stacked = jnp.concatenate([c_all.reshape(N_DEV, d), c_ctx.reshape(1, d), jnp.zeros((16 - N_DEV - 1, d), F32)], 0)

    def body(c_ref, o_ref, ds_ref):
        v = c_ref[...]
        row = lax.broadcasted_iota(jnp.int32, v.shape, 0)
        o_ref[...] = jnp.where(row <= N_DEV, _silu(v), 0.0)
        cc = c_ref[N_DEV:N_DEV + 1, :]
        sg = jax.nn.sigmoid(cc)
        ds_ref[...] = sg * (1.0 + cc * (1.0 - sg))

    return pl.pallas_call(body, name="cond_silu", out_shape=[jax.ShapeDtypeStruct((16, d), F32), jax.ShapeDtypeStruct((1, d), F32)])(stacked)


def _flat_pack(arrs):
    flat = jnp.concatenate([a.reshape(-1) for a in arrs])
    n = flat.shape[0]
    rows = -(-n // LANE)
    rows = -(-rows // 8) * 8
    return jnp.concatenate([flat, jnp.zeros((rows * LANE - n,), F32)]).reshape(rows, LANE)


def _flat_unpack(packed, shapes):
    flat = packed.reshape(-1)
    out, o = [], 0
    for s in shapes:
        n = int(np.prod(s))
        out.append(flat[o:o + n].reshape(s))
        o += n
    return out


XCH_ORDER = ["w_ffn_in", "w_o", "w_in", "w_ffn_out"]
SMALL = ["b_ada", "ret_decay_logit", "dn_a_log", "dn_dt_bias", "dn_norm_w", "att_qn_w", "att_kn_w", "ln1_w", "ln1_b", "ln2_w", "ln2_b"]
OUT_ORDER = ['c_ctx', 'w_ada', 'b_ada', 'w_in', 'ret_decay_logit', 'dn_conv_w', 'dn_a_log', 'dn_dt_bias', 'dn_norm_w', 'att_qn_w',
             'att_kn_w', 'w_o', 'ln1_w', 'ln1_b', 'w_ffn_in', 'w_ffn_out', 'ln2_w', 'ln2_b']


def kernel(x, c, ctx, c_ctx, w_ada, b_ada, w_in, ret_decay_logit, dn_conv_w, dn_a_log, dn_dt_bias, dn_norm_w, att_qn_w, att_kn_w, w_o, ln1_w, ln1_b, w_ffn_in, w_ffn_out, ln2_w, ln2_b, loss_target, m_c_ctx, m_w_ada, m_b_ada, m_w_in, m_ret_decay_logit, m_dn_conv_w, m_dn_a_log, m_dn_dt_bias, m_dn_norm_w, m_att_qn_w, m_att_kn_w, m_w_o, m_ln1_w, m_ln1_b, m_w_ffn_in, m_w_ffn_out, m_ln2_w, m_ln2_b, v_c_ctx, v_w_ada, v_b_ada, v_w_in, v_ret_decay_logit, v_dn_conv_w, v_dn_a_log, v_dn_dt_bias, v_dn_norm_w, v_att_qn_w, v_att_kn_w, v_w_o, v_ln1_w, v_ln1_b, v_w_ffn_in, v_w_ffn_out, v_ln2_w, v_ln2_b):
    weights = dict(c_ctx=c_ctx, w_ada=w_ada, b_ada=b_ada, w_in=w_in, ret_decay_logit=ret_decay_logit, dn_conv_w=dn_conv_w,
                   dn_a_log=dn_a_log, dn_dt_bias=dn_dt_bias, dn_norm_w=dn_norm_w, att_qn_w=att_qn_w, att_kn_w=att_kn_w, w_o=w_o,
                   ln1_w=ln1_w, ln1_b=ln1_b, w_ffn_in=w_ffn_in, w_ffn_out=w_ffn_out, ln2_w=ln2_w, ln2_b=ln2_b)
    mom1 = dict(c_ctx=m_c_ctx, w_ada=m_w_ada, b_ada=m_b_ada, w_in=m_w_in, ret_decay_logit=m_ret_decay_logit, dn_conv_w=m_dn_conv_w,
                dn_a_log=m_dn_a_log, dn_dt_bias=m_dn_dt_bias, dn_norm_w=m_dn_norm_w, att_qn_w=m_att_qn_w, att_kn_w=m_att_kn_w,
                w_o=m_w_o, ln1_w=m_ln1_w, ln1_b=m_ln1_b, w_ffn_in=m_w_ffn_in, w_ffn_out=m_w_ffn_out, ln2_w=m_ln2_w, ln2_b=m_ln2_b)
    mom2 = dict(c_ctx=v_c_ctx, w_ada=v_w_ada, b_ada=v_b_ada, w_in=v_w_in, ret_decay_logit=v_ret_decay_logit, dn_conv_w=v_dn_conv_w,
                dn_a_log=v_dn_a_log, dn_dt_bias=v_dn_dt_bias, dn_norm_w=v_dn_norm_w, att_qn_w=v_att_qn_w, att_kn_w=v_att_kn_w,
                w_o=v_w_o, ln1_w=v_ln1_w, ln1_b=v_ln1_b, w_ffn_in=v_w_ffn_in, w_ffn_out=v_w_ffn_out, ln2_w=v_ln2_w, ln2_b=v_ln2_b)
    depth, d, n_ada = w_ada.shape
    me = _my_index()
    x2, ctx2, tgt2 = x[0], ctx[0], loss_target[0]

    n_ctx, n_lat = ctx2.shape[0], x2.shape[0]
    proj_w = w_in.shape[2] * N_DEV
    big = dict(w_in=w_in.astype(BF16), w_o=w_o.astype(BF16), w_ffn_in=w_ffn_in.astype(BF16), w_ffn_out=w_ffn_out.astype(BF16))

    def shards_of(i):
        return [big[k][i] for k in XCH_ORDER]

    def assemble(gathered):
        g = dict(zip(XCH_ORDER, gathered))
        return dict(w_in=_pad_in_proj(jnp.moveaxis(g["w_in"], 0, 1).reshape(d, proj_w)), w_o=g["w_o"].reshape(d, d),
                    w_ffn_in=g["w_ffn_in"], w_ffn_out=g["w_ffn_out"].reshape(-1, d))

    def grad_slices(glike):
        g = dict(w_in=jnp.moveaxis(_unpad_in_proj(glike["w_in"]).reshape(d, N_DEV, -1), 1, 0), w_o=glike["w_o"].reshape(N_DEV, -1, d),
                 w_ffn_in=glike["w_ffn_in"], w_ffn_out=glike["w_ffn_out"].reshape(N_DEV, -1, d))
        return [g[k] for k in XCH_ORDER]

    first = exchange("gather_first", [c, dn_conv_w] + shards_of(0), ['gather'] * (2 + len(XCH_ORDER)))
    c_all, conv_all, gathered = first[0], first[1], first[2:]
    conv_full = jnp.moveaxis(conv_all, 0, 2).reshape(depth, DN_CONV_K, -1)
    conv_w8 = jnp.concatenate([conv_full, jnp.zeros((depth, 8 - DN_CONV_K, conv_full.shape[2]), F32)], 1)

    cond16, dsilu_ctx = _silu_rows(c_all, c_ctx)
    b_loc = lax.dynamic_slice_in_dim(b_ada, me * n_ada, n_ada, axis=1)[:, None, :]
    mod_loc = _ada_forward(cond16, w_ada, b_loc)
    mod_all, = exchange("gather_mods", [mod_loc], ['gather'])
    mod_full = jnp.moveaxis(mod_all, 0, 2).reshape(depth, 16, N_DEV * n_ada)
    mods = jnp.stack([mod_full[:, N_DEV], lax.dynamic_index_in_dim(mod_full, me, 1, keepdims=False)], 1).reshape(depth, 2, 6, d)

    sm_all = dict(ret_decay_logit=ret_decay_logit, dn_a_log=dn_a_log, dn_dt_bias=dn_dt_bias, dn_norm_w=dn_norm_w, att_qn_w=att_qn_w,
                  att_kn_w=att_kn_w, ln1_w=ln1_w, ln1_b=ln1_b, ln2_w=ln2_w, ln2_b=ln2_b, conv_w8=conv_w8)
    cosf, sins = _rope_tables(n_lat, n_ctx)
    xs = jnp.concatenate([ctx2, x2], 0)
    vjps = []
    for i in range(depth):
        wt = assemble(gathered)
        like = {k: jnp.zeros(a.shape, F32) for k, a in wt.items()}
        more = i + 1 < depth
        carry = [jnp.zeros(a.shape, F32) for a in jax.eval_shape(grad_slices, like)] if more else []
        layer = functools.partial(_layer, i, depth, n_ctx, cosf, sins, wt, shards_of(i + 1) if more else [])
        (xs, _), vjp_i, gathered = jax.vjp(layer, xs, mods[i], like, {k: v[i] for k, v in sm_all.items()}, carry, has_aux=True)
        vjps.append(vjp_i)
    loss, vjp_loss = jax.vjp(lambda y: loss_head(y, tgt2), xs)
    loss = lax.psum(loss, ("x", "y", "c"))

    g, = vjp_loss(jnp.ones((), F32))
    payload, received = [], [None] * depth
    gmods, gsm = [None] * depth, [None] * depth
    for i in reversed(range(depth)):
        g, gmods[i], glike, gsm[i], got = vjps[i]((g, payload))
        if i + 1 < depth:
            received[i + 1] = got
        payload = grad_slices(glike)
    gx = g[n_ctx:]

    gm = jnp.stack(gmods).reshape(depth, 2, 6 * d)
    small_parts = {k: jnp.stack([gs[k] for gs in gsm]) for k in gsm[0]}
    small_parts["b_ada"] = gm[:, 0] + gm[:, 1]
    small_pack = _flat_pack([small_parts[k] for k in SMALL])
    g_conv = jnp.moveaxis(small_parts["conv_w8"][:, :DN_CONV_K].reshape(depth, DN_CONV_K, N_DEV, -1), 2, 0)
    got = exchange("exchange_last", [gm, small_pack, g_conv] + payload, ['gather', 'gather'] + ['a2a'] * (1 + len(payload)))
    gm_all, small_all, conv_got = got[0], got[1], got[2]
    received[0] = got[3:]
    by_name = {k: jnp.stack([received[l][n] for l in range(depth)], 1) for n, k in enumerate(XCH_ORDER)}

    ctx_sum = gm_all[0, :, 0]
    for p in range(1, N_DEV):
        ctx_sum = ctx_sum + gm_all[p, :, 0]
    dm_full = jnp.concatenate([jnp.moveaxis(gm_all[:, :, 1], 0, 1), ctx_sum[:, None], jnp.zeros((depth, 16 - N_DEV - 1, 6 * d), F32)], 1)
    dm16 = lax.dynamic_slice_in_dim(dm_full, me * n_ada, n_ada, axis=2)
    g_w_ada, dcond = _ada_backward(cond16, w_ada, dm16)
    dcond_all, = exchange("gather_dcond", [dcond[N_DEV:N_DEV + 1]], ['gather'])

    out = {}

    def update(name, parts3, shape, scale=None):
        w2 = weights[name].reshape(parts3.shape[1:])
        g, dl, nm, nv = adamw("adamw_" + name, parts3, w2, mom1[name].reshape(w2.shape), mom2[name].reshape(w2.shape), scale)
        out[name] = tuple(a.reshape(shape) for a in (g, dl, nm, nv))

    update("w_in", by_name["w_in"].reshape(N_DEV, depth * d, -1), w_in.shape)
    update("w_o", by_name["w_o"].reshape(N_DEV, -1, d), w_o.shape)
    update("w_ffn_in", by_name["w_ffn_in"].reshape(N_DEV, depth * d, -1), w_ffn_in.shape)
    update("w_ffn_out", by_name["w_ffn_out"].reshape(N_DEV, -1, d), w_ffn_out.shape)
    update("dn_conv_w", conv_got.reshape(N_DEV, depth * DN_CONV_K, -1), dn_conv_w.shape)
    update("w_ada", g_w_ada.reshape(1, depth * d, n_ada), w_ada.shape)
    update("c_ctx", dcond_all.reshape(N_DEV, 1, d), c_ctx.shape, scale=dsilu_ctx)

    small_shapes = [weights[k].shape for k in SMALL]
    w_pack, m_pack, v_pack = (_flat_pack([src[k] for k in SMALL]) for src in (weights, mom1, mom2))
    packs = adamw("adamw_small", small_all, w_pack, m_pack, v_pack)
    for k, vals in zip(SMALL, zip(*[_flat_unpack(pk, small_shapes) for pk in packs])):
        out[k] = vals

    res = [loss, gx[None]]
    for slot in range(4):
        res += [out[k][slot] for k in OUT_ORDER]
    return tuple(res)
```

```python
import functools
import math

import jax
import jax.numpy as jnp
import numpy as np
from jax import lax
from jax.experimental import pallas as pl
from jax.experimental.pallas import tpu as pltpu

F32 = jnp.float32
BF16 = jnp.bfloat16
HI = lax.Precision.HIGHEST

HEAD_DIM = 128
RET_HEADS = 4
DN_HEADS = 4
ATT_HEADS = 8
ATT_KV_HEADS = 2
RET_CHUNK = 128
DN_CHUNK = 64
DN_CONV_K = 5
GRID_W = 64
ROPE_THETA = 10000.0
EPS = 1e-6
ADAM_LR = 0.001
ADAM_B1 = 0.9
ADAM_B2 = 0.999
ADAM_EPS = 1e-08
ADAM_WD = 0.01
ADAM_STEP = 10
N_DEV = 8
LANE = 128
LN2 = math.log(2.0)
LOG2E = 1.0 / LN2
ROW_TILE = 256
MESH = pl.DeviceIdType.MESH


def _tile(n, cap, unit=LANE):
    best = None
    for t in range(unit, min(n, cap) + 1, unit):
        if n % t == 0:
            best = t
    return n if best is None else best


def _op(fwd, bwd):
    @jax.custom_vjp
    def op(*args):
        return fwd(*args)[0]
    op.defvjp(fwd, bwd)
    return op


def _cast(x, exact):
    return x if exact else x.astype(BF16)


def _dot(a, b, dims, exact):
    return lax.dot_general(_cast(a, exact), _cast(b, exact), (dims, ((), ())),
                           precision=HI if exact else None, preferred_element_type=F32)


def mm_nn(name, a, b, exact=False, tm_cap=1024, tn_cap=1408, tk_cap=2048):
    m, k = a.shape
    slotted = b.ndim == 3
    if slotted:
        s, _, ns = b.shape
        n = s * ns
        tn = _tile(ns, tn_cap)
        per = ns // tn
    else:
        n = b.shape[1]
        tn = _tile(n, tn_cap)
    tm = _tile(m, tm_cap, 8)
    tk = _tile(k, tk_cap)
    nk = k // tk

    def body(a_ref, b_ref, o_ref):
        kk = pl.program_id(2)

        @pl.when(kk == 0)
        def _():
            o_ref[...] = jnp.zeros_like(o_ref)
        o_ref[...] += _dot(a_ref[...], b_ref[...], ((1,), (0,)), exact)

    if slotted:
        b_spec = pl.BlockSpec((None, tk, tn), lambda i, j, kk: (j // per, kk, j % per))
    else:
        b_spec = pl.BlockSpec((tk, tn), lambda i, j, kk: (kk, j))
    return pl.pallas_call(
        body, name=name, grid=(m // tm, n // tn, nk),
        in_specs=[pl.BlockSpec((tm, tk), lambda i, j, kk: (i, kk)), b_spec],
        out_specs=pl.BlockSpec((tm, tn), lambda i, j, kk: (i, j)),
        out_shape=jax.ShapeDtypeStruct((m, n), F32),
        compiler_params=pltpu.CompilerParams(dimension_semantics=("parallel", "parallel", "arbitrary")),
    )(a, b)


def mm_nt(name, a, b, exact=False, tm_cap=1024, tn_cap=2048, tk_cap=1408):
    m, n = a.shape
    slotted = b.ndim == 3
    if slotted:
        s, k, ns = b.shape
        tk = _tile(ns, tk_cap)
        per = ns // tk
    else:
        k = b.shape[0]
        tk = _tile(n, tk_cap)
    tm = _tile(m, tm_cap, 8)
    tn = _tile(k, tn_cap)
    nk = n // tk

    def body(a_ref, b_ref, o_ref):
        kk = pl.program_id(2)

        @pl.when(kk == 0)
        def _():
            o_ref[...] = jnp.zeros_like(o_ref)
        o_ref[...] += _dot(a_ref[...], b_ref[...], ((1,), (1,)), exact)

    if slotted:
        b_spec = pl.BlockSpec((None, tn, tk), lambda i, j, kk: (kk // per, j, kk % per))
    else:
        b_spec = pl.BlockSpec((tn, tk), lambda i, j, kk: (j, kk))
    return pl.pallas_call(
        body, name=name, grid=(m // tm, k // tn, nk),
        in_specs=[pl.BlockSpec((tm, tk), lambda i, j, kk: (i, kk)), b_spec],
        out_specs=pl.BlockSpec((tm, tn), lambda i, j, kk: (i, j)),
        out_shape=jax.ShapeDtypeStruct((m, k), F32),
        compiler_params=pltpu.CompilerParams(dimension_semantics=("parallel", "parallel", "arbitrary")),
    )(a, b)


def mm_tn(name, a, b, slots=None, exact=False, tm_cap=1024, tn_cap=1408, tk_cap=1024):
    m, k = a.shape
    n = b.shape[1]
    if slots is not None:
        s, ns = slots
        tn = _tile(ns, tn_cap)
        per = ns // tn
        out_shape = jax.ShapeDtypeStruct((s, k, ns), F32)
    else:
        tn = _tile(n, tn_cap)
        out_shape = jax.ShapeDtypeStruct((k, n), F32)
    tm = _tile(k, tm_cap)
    tk = _tile(m, tk_cap, 8)
    nk = m // tk

    def body(a_ref, b_ref, o_ref):
        kk = pl.program_id(2)

        @pl.when(kk == 0)
        def _():
            o_ref[...] = jnp.zeros_like(o_ref)
        o_ref[...] += _dot(a_ref[...], b_ref[...], ((0,), (0,)), exact)

    if slots is not None:
        o_spec = pl.BlockSpec((None, tm, tn), lambda i, j, kk: (j // per, i, j % per))
    else:
        o_spec = pl.BlockSpec((tm, tn), lambda i, j, kk: (i, j))
    return pl.pallas_call(
        body, name=name, grid=(k // tm, n // tn, nk),
        in_specs=[pl.BlockSpec((tk, tm), lambda i, j, kk: (kk, i)), pl.BlockSpec((tk, tn), lambda i, j, kk: (kk, j))],
        out_specs=o_spec, out_shape=out_shape,
        compiler_params=pltpu.CompilerParams(dimension_semantics=("parallel", "parallel", "arbitrary")),
    )(a, b)


def _split_cols(y, widths):
    offs = np.cumsum([0] + list(widths))
    return tuple(y[:, int(a):int(b)] for a, b in zip(offs[:-1], offs[1:]))


def linear(name, a, w, like, split=None):
    def fwd(a, w, like):
        y = mm_nn(name + "_fwd", a, w)
        return (y if split is None else _split_cols(y, split)), (a, w)

    def bwd(res, dy):
        a, w = res
        if split is not None:
            dy = jnp.concatenate(dy, axis=1)
        da = mm_nt(name + "_dx", dy, w)
        slots = (w.shape[0], w.shape[2]) if w.ndim == 3 else None
        dw = mm_tn(name + "_dw", a, dy, slots=slots)
        return da, jnp.zeros_like(w), dw

    return _op(fwd, bwd)(a, w, like)


def rowwise(name, f, rows, row_modes, row_diff, vecs, vec_kinds, vec_diff, out_defs, ncol=1, nctx=0, tm=None):
    nr, nv = len(rows), len(vecs)
    t = rows[0].shape[0]
    tm = min(ROW_TILE, t) if tm is None else tm
    assert t % tm == 0 and (tm == min(ROW_TILE, t) or 'seg' not in vec_kinds)
    nrow = t // tm

    def row_spec(a, mode):
        w = a.shape[1]
        if mode == 'j':
            return pl.BlockSpec((tm, w // ncol), lambda i, j: (i, j))
        return pl.BlockSpec((tm, w), lambda i, j: (i, 0))

    def vec_spec(a, kind):
        nd = a.ndim
        if kind == 'shared':
            return pl.BlockSpec(a.shape, lambda i, j: (0,) * nd)
        return pl.BlockSpec((None,) + a.shape[1:], lambda i, j: ((i >= nctx).astype(jnp.int32),) + (0,) * (nd - 1))

    in_specs = [row_spec(a, m) for a, m in zip(rows, row_modes)] + [vec_spec(a, k) for a, k in zip(vecs, vec_kinds)]
    out_specs = [pl.BlockSpec((tm, w), lambda i, j: (i, j)) for w, _ in out_defs]
    out_shape = [jax.ShapeDtypeStruct((t, ncol * w), dt) for w, dt in out_defs]
    params = pltpu.CompilerParams(dimension_semantics=("arbitrary", "arbitrary"))

    def fwd_call(*args):
        def body(*refs):
            vals = [r[...] for r in refs[:nr + nv]]
            outs = f(*vals)
            for o_ref, o in zip(refs[nr + nv:], outs):
                o_ref[...] = o.astype(o_ref.dtype)
        return pl.pallas_call(body, name=name + "_fwd", grid=(nrow, ncol), in_specs=in_specs, out_specs=out_specs,
                              out_shape=out_shape, compiler_params=params)(*args)

    diff_idx = [i for i in range(nr) if row_diff[i]] + [nr + i for i in range(nv) if vec_diff[i]]
    d_rows = [i for i in range(nr) if row_diff[i]]
    d_vecs = [i for i in range(nv) if vec_diff[i]]

    def bwd_call(args, cts):
        n_in = nr + nv + len(out_defs)

        def body(*refs):
            i, j = pl.program_id(0), pl.program_id(1)
            vals = [r[...] for r in refs[:nr + nv]]
            ct = tuple(r[...] for r in refs[nr + nv:n_in])

            def g(*dvals):
                full = list(vals)
                for idx, v in zip(diff_idx, dvals):
                    full[idx] = v
                return tuple(f(*full))

            outs, vjp = jax.vjp(g, *[vals[idx] for idx in diff_idx])
            grads = vjp(tuple(c.astype(o.dtype) for c, o in zip(ct, outs)))
            out_refs = refs[n_in:]
            for p, _ in enumerate(d_rows):
                out_refs[p][...] = grads[p].astype(out_refs[p].dtype)
            for p, vi in enumerate(d_vecs):
                ref = out_refs[len(d_rows) + p]
                if vec_kinds[vi] == 'shared':
                    first = jnp.logical_and(i == 0, j == 0)
                else:
                    first = jnp.logical_and(jnp.logical_or(i == 0, i == nctx), j == 0)

                @pl.when(first)
                def _():
                    ref[...] = jnp.zeros_like(ref)
                ref[...] += grads[len(d_rows) + p].astype(F32)

        ct_specs = [pl.BlockSpec((tm, w), lambda i, j: (i, j)) for w, _ in out_defs]
        g_specs = [row_spec(rows[i], row_modes[i]) for i in d_rows] + [vec_spec(vecs[i], vec_kinds[i]) for i in d_vecs]
        g_shape = [jax.ShapeDtypeStruct(rows[i].shape, rows[i].dtype) for i in d_rows] + \
                  [jax.ShapeDtypeStruct(vecs[i].shape, F32) for i in d_vecs]
        return pl.pallas_call(body, name=name + "_bwd", grid=(nrow, ncol), in_specs=in_specs + ct_specs, out_specs=g_specs,
                              out_shape=g_shape, compiler_params=params)(*args, *cts)

    def fwd(*args):
        return tuple(fwd_call(*args)), args

    def bwd(args, cts):
        grads = bwd_call(args, cts)
        out = [None] * (nr + nv)
        for p, idx in enumerate(diff_idx):
            out[idx] = grads[p]
        for idx in range(nr + nv):
            if out[idx] is None:
                out[idx] = jnp.zeros_like(args[idx])
        return tuple(out)

    for i in d_rows:
        assert row_modes[i] == 'j' or ncol == 1
    return _op(fwd, bwd)(*rows, *vecs)


def _silu(x):
    return x * jax.nn.sigmoid(x)


def _roll_half(x):
    return pltpu.roll(x, HEAD_DIM // 2, axis=1)


@jax.custom_vjp
def _rope(x, cosf, sins):
    return x * cosf + _roll_half(x) * sins


def _rope_fwd(x, cosf, sins):
    return _rope(x, cosf, sins), (cosf, sins)


def _rope_bwd(res, dy):
    cosf, sins = res
    return dy * cosf + _roll_half(dy * sins), jnp.zeros_like(cosf), jnp.zeros_like(sins)


_rope.defvjp(_rope_fwd, _rope_bwd)


def _f_mod(x, shift, scale):
    return (x * (1.0 + scale) + shift,)


def _layer_norm(z, w, b):
    mu = jnp.mean(z, -1, keepdims=True)
    zc = z - mu
    var = jnp.mean(zc * zc, -1, keepdims=True)
    return zc * lax.rsqrt(var + EPS) * w + b


def _f_norm_mod(alpha, x, t, gate, w, b, shift, scale):
    xn = _layer_norm(alpha * x + gate * t, w, b)
    return xn, xn * (1.0 + scale) + shift


def _f_norm(alpha, x, t, gate, w, b):
    return (_layer_norm(alpha * x + gate * t, w, b),)


def _f_ret_prep(q, k, cosf, sins):
    return _rope(q, cosf, sins), _rope(k, cosf, sins) * HEAD_DIM ** -0.5


def _rms(x):
    return x * lax.rsqrt(jnp.mean(x * x, -1, keepdims=True) + EPS)


def _f_gated_out(of, ob, gate):
    return (_rms(of + ob) * _silu(gate),)


def _f_gated_out_w(of, ob, gate, w):
    return (_rms(of + ob) * w * _silu(gate),)


def _l2n(x):
    return x * lax.rsqrt(jnp.sum(x * x, -1, keepdims=True) + EPS)


def _f_dn_prep(cq, ck, cv):
    return _l2n(_silu(cq)) * HEAD_DIM ** -0.5, _l2n(_silu(ck)), _silu(cv)


def _f_qk_norm(x, cosf, sins, w):
    return (_rope(_rms(x) * w, cosf, sins),)


def _f_gates(ab, alog, dtb):
    tm = ab.shape[0]
    lane = lax.broadcasted_iota(jnp.int32, (1, LANE), 1)
    g = -jnp.exp(alog) * jax.nn.softplus(ab + dtb)
    beta = jax.nn.sigmoid(ab)
    r = lax.broadcasted_iota(jnp.int32, (tm, tm), 0)
    c = lax.broadcasted_iota(jnp.int32, (tm, tm), 1)
    same = (r // DN_CHUNK) == (c // DN_CHUNK)
    lower = jnp.where(jnp.logical_and(same, c <= r), 1.0, 0.0).astype(F32)
    upper = jnp.where(jnp.logical_and(same, c >= r), 1.0, 0.0).astype(F32)
    gl = jnp.dot(lower, g, precision=HI, preferred_element_type=F32)
    gu = jnp.dot(upper, g, precision=HI, preferred_element_type=F32)
    out = jnp.where(lane < DN_HEADS, gl, jnp.where(lane < 2 * DN_HEADS, gu, jnp.where(lane < 4 * DN_HEADS, beta, 0.0)))
    return (out,)


def swiglu(name, u, tn_cap=1408):
    t, f2 = u.shape
    ff = f2 // 2
    tn = _tile(ff, tn_cap)
    nc = ff // tn
    tm = min(ROW_TILE, t)

    def fwd_call(u):
        def body(g_ref, u_ref, o_ref):
            o_ref[...] = _silu(g_ref[...]) * u_ref[...]
        return pl.pallas_call(
            body, name=name + "_fwd", grid=(t // tm, nc),
            in_specs=[pl.BlockSpec((tm, tn), lambda i, j: (i, j)), pl.BlockSpec((tm, tn), lambda i, j: (i, j + nc))],
            out_specs=pl.BlockSpec((tm, tn), lambda i, j: (i, j)), out_shape=jax.ShapeDtypeStruct((t, ff), F32),
            compiler_params=pltpu.CompilerParams(dimension_semantics=("parallel", "parallel")))(u, u)

    def bwd_call(u, da):
        tb = min(64, t)

        def body(u_ref, da_ref, o_ref):
            gate, up, d = u_ref[:, :ff], u_ref[:, ff:], da_ref[...]
            sg = jax.nn.sigmoid(gate)
            o_ref[:, :ff] = d * up * (sg * (1.0 + gate * (1.0 - sg)))
            o_ref[:, ff:] = d * (gate * sg)
        return pl.pallas_call(
            body, name=name + "_bwd", grid=(t // tb,),
            in_specs=[pl.BlockSpec((tb, f2), lambda i: (i, 0)), pl.BlockSpec((tb, ff), lambda i: (i, 0))],
            out_specs=pl.BlockSpec((tb, f2), lambda i: (i, 0)), out_shape=jax.ShapeDtypeStruct((t, f2), F32),
            compiler_params=pltpu.CompilerParams(dimension_semantics=("parallel",)))(u, da)

    return _op(lambda u: (fwd_call(u), (u,)), lambda res, da: (bwd_call(res[0], da),))(u)


HALO = 8


def _conv_specs(t, c, tm, tc):
    nb8 = t // HALO
    per = tm // HALO
    cur = pl.BlockSpec((tm, tc), lambda j, i: (i, j))
    prev = pl.BlockSpec((HALO, tc), lambda j, i: (jnp.maximum(i * per - 1, 0), j))
    nxt = pl.BlockSpec((HALO, tc), lambda j, i: (jnp.minimum((i + 1) * per, nb8 - 1), j))
    return cur, prev, nxt


def _extended(prev_ref, cur_ref, next_ref, ext_ref, i, nrow, nctx, tm):
    has_prev = jnp.logical_and(i != 0, i != nctx)
    has_next = jnp.logical_and(i != nrow - 1, i != nctx - 1)
    ext_ref[0:HALO, :] = jnp.where(has_prev, prev_ref[...], 0.0)
    ext_ref[HALO:HALO + tm, :] = cur_ref[...]
    ext_ref[HALO + tm:, :] = jnp.where(has_next, next_ref[...], 0.0)


def _conv_call(name, x, w8, nctx, flip):
    t, c = x.shape
    tm = min(ROW_TILE, t)
    tc = _tile(c, 512)
    nrow = t // tm
    pad = DN_CONV_K // 2

    def body(cur_ref, prev_ref, next_ref, w_ref, o_ref, ext_ref):
        i = pl.program_id(1)
        _extended(prev_ref, cur_ref, next_ref, ext_ref, i, nrow, nctx, tm)
        acc = jnp.zeros((tm, tc), F32)
        for j in range(DN_CONV_K):
            wj = w_ref[(DN_CONV_K - 1 - j) if flip else j, :][None, :]
            acc = acc + wj * ext_ref[HALO - pad + j:HALO - pad + j + tm, :]
        o_ref[...] = acc

    cur, prev, nxt = _conv_specs(t, c, tm, tc)
    return pl.pallas_call(
        body, name=name, grid=(c // tc, nrow),
        in_specs=[cur, prev, nxt, pl.BlockSpec((8, tc), lambda j, i: (0, j))],
        out_specs=pl.BlockSpec((tm, tc), lambda j, i: (i, j)), out_shape=jax.ShapeDtypeStruct((t, c), F32),
        scratch_shapes=[pltpu.VMEM((tm + 2 * HALO, tc), F32)],
        compiler_params=pltpu.CompilerParams(dimension_semantics=("arbitrary", "arbitrary")))(x, x, x, w8)


def _conv_dw_call(name, x, dy, nctx):
    t, c = x.shape
    tm = min(ROW_TILE, t)
    tc = _tile(c, 512)
    nrow = t // tm
    pad = DN_CONV_K // 2

    def body(cur_ref, prev_ref, next_ref, dy_ref, o_ref, ext_ref):
        i = pl.program_id(1)
        _extended(prev_ref, cur_ref, next_ref, ext_ref, i, nrow, nctx, tm)

        @pl.when(i == 0)
        def _():
            o_ref[...] = jnp.zeros_like(o_ref)
        dy = dy_ref[...]
        rows = [jnp.sum(dy * ext_ref[HALO - pad + j:HALO - pad + j + tm, :], axis=0, keepdims=True) for j in range(DN_CONV_K)]
        rows += [jnp.zeros((1, tc), F32)] * (8 - DN_CONV_K)
        o_ref[...] += jnp.concatenate(rows, axis=0)

    cur, prev, nxt = _conv_specs(t, c, tm, tc)
    return pl.pallas_call(
        body, name=name, grid=(c // tc, nrow),
        in_specs=[cur, prev, nxt, pl.BlockSpec((tm, tc), lambda j, i: (i, j))],
        out_specs=pl.BlockSpec((8, tc), lambda j, i: (0, j)), out_shape=jax.ShapeDtypeStruct((8, c), F32),
        scratch_shapes=[pltpu.VMEM((tm + 2 * HALO, tc), F32)],
        compiler_params=pltpu.CompilerParams(dimension_semantics=("arbitrary", "arbitrary")))(x, x, x, dy)


def short_conv(name, x, w8, nctx, split):
    def fwd(x, w8):
        return _split_cols(_conv_call(name + "_fwd", x, w8, nctx, False), split), (x, w8)

    def bwd(res, dy):
        x, w8 = res
        dy = jnp.concatenate(dy, axis=1)
        return _conv_call(name + "_dx", dy, w8, nctx, True), _conv_dw_call(name + "_dw", x, dy, nctx)

    return _op(fwd, bwd)(x, w8)


def _split_bf16(x):
    hi = x.astype(BF16)
    return hi, (x - hi.astype(F32)).astype(BF16)


def _mm_raw(a, b, form, exact):
    dims = {"nn": ((2,), (1,)), "nt": ((2,), (2,)), "tn": ((1,), (1,))}[form]
    dg = lambda p, q: lax.dot_general(p, q, (dims, ((0,), (0,))), preferred_element_type=F32)
    if not exact:
        return dg(_cast(a, False), _cast(b, False))
    ah, al = _split_bf16(a)
    bh, bl = _split_bf16(b)
    return dg(ah, bh) + (dg(ah, bl) + dg(al, bh))


@functools.partial(jax.custom_vjp, nondiff_argnums=(2, 3))
def _mm(a, b, form, exact=False):
    return _mm_raw(a, b, form, exact)


def _mm_fwd(a, b, form, exact):
    return _mm_raw(a, b, form, exact), (a, b)


def _mm_bwd(form, exact, res, dc):
    a, b = res
    if form == "nn":
        return _mm_raw(dc, b, "nt", exact), _mm_raw(a, dc, "tn", exact)
    if form == "nt":
        return _mm_raw(dc, b, "nn", exact), _mm_raw(dc, a, "tn", exact)
    return _mm_raw(b, dc, "nt", exact), _mm_raw(a, dc, "nn", exact)


_mm.defvjp(_mm_fwd, _mm_bwd)


def chunk_scan(name, f, chunk, heads, ncc, q, k, v, gargs_f, gargs_b, pargs):
    t = q.shape[0]
    nc = t // chunk
    g_all = 2 * heads
    d = HEAD_DIM
    hw = heads * d
    ng, npar = len(gargs_f), len(pargs)

    def pos_b(n):
        return jnp.where(n < ncc, ncc - 1 - n, ncc + nc - 1 - n)

    def specs(rev_visit):
        def vis(n):
            return (nc - 1 - n) if rev_visit else n
        qf = pl.BlockSpec((chunk, hw), lambda n: (vis(n), 0))
        qb = pl.BlockSpec((chunk, hw), lambda n: (pos_b(vis(n)), 0))
        gf = [pl.BlockSpec((heads, None) + a.shape[2:], lambda n: (0, vis(n), 0, 0)) for a in gargs_f]
        gb = [pl.BlockSpec((heads, None) + a.shape[2:], lambda n: (0, pos_b(vis(n)), 0, 0)) for a in gargs_b]
        ps = [pl.BlockSpec(a.shape, lambda n: (0, 0, 0)) for a in pargs]
        ss = pl.BlockSpec((g_all, None, d, d), lambda n: (0, vis(n), 0, 0))
        return qf, qb, gf, gb, ps, ss

    params = pltpu.CompilerParams(dimension_semantics=("arbitrary",))
    n_in = 6 + 2 * ng + npar

    def cols(h):
        return slice(h * d, (h + 1) * d)

    def stacked_inputs(refs):
        qkv = [jnp.stack([refs[i][:, cols(h)] for h in range(heads)] + [refs[3 + i][:, cols(h)] for h in range(heads)])
               for i in range(3)]
        gar = [jnp.concatenate([refs[6 + i][...], refs[6 + ng + i][...]], axis=0) for i in range(ng)]
        par = [r[...] for r in refs[6 + 2 * ng:n_in]]
        return qkv + gar + par

    def scatter(vals, refs_f, refs_b):
        for r_f, r_b, val in zip(refs_f, refs_b, vals):
            for h in range(heads):
                r_f[:, cols(h)] = val[h]
                r_b[:, cols(h)] = val[heads + h]

    def fwd_call(q, k, v, *rest):
        qf, qb, gf, gb, ps, ss = specs(False)

        def body(*refs):
            of_ref, ob_ref, sp_ref, s_ref = refs[n_in:]

            @pl.when(pl.program_id(0) == 0)
            def _():
                s_ref[...] = jnp.zeros_like(s_ref)
            s_prev = s_ref[...]
            sp_ref[...] = s_prev
            o, s_new = f(heads, *stacked_inputs(refs), s_prev)
            scatter([o], [of_ref], [ob_ref])
            s_ref[...] = s_new

        oshape = jax.ShapeDtypeStruct((t, hw), F32)
        return pl.pallas_call(
            body, name=name + "_fwd", grid=(nc,), in_specs=[qf] * 3 + [qb] * 3 + gf + gb + ps, out_specs=[qf, qb, ss],
            out_shape=[oshape, oshape, jax.ShapeDtypeStruct((g_all, nc, d, d), F32)],
            scratch_shapes=[pltpu.VMEM((g_all, d, d), F32)], compiler_params=params)(q, k, v, q, k, v, *rest)

    def bwd_call(q, k, v, rest, s_prev, do_f, do_b):
        qf, qb, gf, gb, ps, ss = specs(True)

        def body(*refs):
            sp_ref, dof_ref, dob_ref = refs[n_in:n_in + 3]
            outs = refs[n_in + 3:]
            dqkv_f, dqkv_b = outs[0:3], outs[3:6]
            dg_f, dg_b = outs[6:6 + ng], outs[6 + ng:6 + 2 * ng]
            dp_refs = outs[6 + 2 * ng:6 + 2 * ng + npar]
            ds_ref = outs[6 + 2 * ng + npar]

            @pl.when(pl.program_id(0) == 0)
            def _():
                ds_ref[...] = jnp.zeros_like(ds_ref)
                for r in dp_refs:
                    r[...] = jnp.zeros_like(r)
            _, vjp = jax.vjp(functools.partial(f, heads), *stacked_inputs(refs), sp_ref[...])
            do = jnp.stack([dof_ref[:, cols(h)] for h in range(heads)] + [dob_ref[:, cols(h)] for h in range(heads)])
            grads = vjp((do, ds_ref[...]))
            scatter(grads[:3], dqkv_f, dqkv_b)
            for r_f, r_b, gr in zip(dg_f, dg_b, grads[3:3 + ng]):
                r_f[...] = gr[:heads]
                r_b[...] = gr[heads:]
            for r, gr in zip(dp_refs, grads[3 + ng:3 + ng + npar]):
                r[...] += gr
            ds_ref[...] = grads[3 + ng + npar]

        dshape = jax.ShapeDtypeStruct((t, hw), F32)
        return pl.pallas_call(
            body, name=name + "_bwd", grid=(nc,), in_specs=[qf] * 3 + [qb] * 3 + gf + gb + ps + [ss, qf, qb],
            out_specs=[qf] * 3 + [qb] * 3 + gf + gb + ps,
            out_shape=[dshape] * 6 + [jax.ShapeDtypeStruct(a.shape, F32) for a in list(gargs_f) + list(gargs_b) + list(pargs)],
            scratch_shapes=[pltpu.VMEM((g_all, d, d), F32)], compiler_params=params)(q, k, v, q, k, v, *rest, s_prev, do_f, do_b)

    def fwd(q, k, v, *rest):
        o_f, o_b, s_prev = fwd_call(q, k, v, *rest)
        return (o_f, o_b), (q, k, v, rest, s_prev)

    def bwd(res, do):
        q, k, v, rest, s_prev = res
        grads = bwd_call(q, k, v, rest, s_prev, do[0], do[1])
        return tuple(grads[i] + grads[3 + i] for i in range(3)) + tuple(grads[6:])

    return _op(fwd, bwd)(q, k, v, *gargs_f, *gargs_b, *pargs)


def _order_masks(heads, c):
    g = 2 * heads
    rev = lax.broadcasted_iota(jnp.int32, (g, c, c), 0) >= heads
    i = lax.broadcasted_iota(jnp.int32, (g, c, c), 1)
    j = lax.broadcasted_iota(jnp.int32, (g, c, c), 2)
    rel = jnp.where(rev, j - i, i - j)
    pos = lax.broadcasted_iota(jnp.int32, (g, c, 1), 1)
    p = jnp.where(lax.broadcasted_iota(jnp.int32, (g, c, 1), 0) >= heads, c - 1 - pos, pos)
    return rel, p


def _f_retention(heads, q, k, v, logit, s):
    c = RET_CHUNK
    lg = jax.nn.log_sigmoid(logit)
    rel, p = _order_masks(heads, c)
    rel = rel.astype(F32)
    p = p.astype(F32)
    decay = jnp.where(rel >= 0, jnp.exp(jnp.maximum(rel, 0.0) * lg), 0.0)
    o_intra = _mm(_mm(q, k, "nt") * decay, v, "nn")
    q_decay = jnp.exp((p + 1.0) * lg)
    k_decay = jnp.exp((c - 1.0 - p) * lg)
    o_inter = _mm(q * q_decay, s, "nn")
    s_new = s * jnp.exp(c * lg) + _mm(k * k_decay, v, "tn")
    return o_intra + o_inter, s_new


def _f_delta(heads, q, k, v, gc, gr, bc, s):
    c = DN_CHUNK
    rel, p = _order_masks(heads, c)
    tri = rel >= 0
    strict = rel > 0
    decay = jnp.where(tri, jnp.exp(jnp.where(tri, gc - gr, 0.0)), 0.0)
    kb = k * bc
    vb = v * bc
    a = jnp.where(strict, _mm(kb, k, "nt") * decay, 0.0)
    pw = -a
    tinv = jnp.where(rel == 0, 1.0, 0.0).astype(F32) + pw
    for _ in range(int(math.log2(c)) - 1):
        pw = _mm(pw, pw, "nn", True)
        tinv = tinv + _mm(tinv, pw, "nn", True)
    eg = jnp.exp(gc)
    w_val = _mm(tinv, vb, "nn")
    k_cum = _mm(tinv, kb * eg, "nn")
    qk = _mm(q, k, "nt") * decay
    g_last = jnp.sum(jnp.where(p == c - 1, gc, 0.0), axis=1, keepdims=True)
    k_g = k * jnp.exp(g_last - gc)
    v_new = w_val - _mm(k_cum, s, "nn")
    o = _mm(q * eg, s, "nn") + _mm(qk, v_new, "nn")
    s_new = s * jnp.exp(g_last) + _mm(k_g, v_new, "tn")
    return o, s_new


def _att_tiles(tq, tk):
    return _tile(tq, 1024, 8), _tile(tk, 1408, 8)


def _grid_ends(grid):
    ids = [pl.program_id(a) for a in range(len(grid))]
    first = functools.reduce(jnp.logical_and, [i == 0 for i in ids])
    last = functools.reduce(jnp.logical_and, [i == n - 1 for i, n in zip(ids, grid)])
    return first, last


def _att_fwd_call(name, q, k, v, rider):
    tq, tk = q.shape[0], k.shape[0]
    h, kvh = q.shape[1] // HEAD_DIM, k.shape[1] // HEAD_DIM
    grp = h // kvh
    bq, bk = _att_tiles(tq, tk)
    nk = tk // bk
    scale = HEAD_DIM ** -0.5
    grid = (h, tq // bq, nk)
    nx = rider.n

    def body(*refs):
        q_ref, k_ref, v_ref = refs[:3]
        o_ref, lse_ref = refs[3 + nx:5 + nx]
        m_sc, l_sc, acc_sc, q2_sc = refs[5 + 2 * nx:9 + 2 * nx]
        rider.run(refs[3:3 + nx], refs[5 + nx:5 + 2 * nx], refs[9 + 2 * nx:], *_grid_ends(grid))
        kj = pl.program_id(2)

        @pl.when(kj == 0)
        def _():
            m_sc[...] = jnp.full_like(m_sc, -jnp.inf)
            l_sc[...] = jnp.zeros_like(l_sc)
            acc_sc[...] = jnp.zeros_like(acc_sc)
            q2_sc[...] = (q_ref[...] * (scale * LOG2E)).astype(q2_sc.dtype)
        s = _dot(q2_sc[...], k_ref[...], ((1,), (1,)), False)
        m_prev = m_sc[...]
        m_new = jnp.maximum(m_prev, jnp.max(s, axis=1, keepdims=True))
        p = jnp.exp2(s - m_new)
        alpha = jnp.exp2(m_prev - m_new)
        l_sc[...] = alpha * l_sc[...] + jnp.sum(p, axis=1, keepdims=True)
        acc_sc[...] = alpha * acc_sc[...] + _dot(p, v_ref[...], ((1,), (0,)), False)
        m_sc[...] = m_new

        @pl.when(kj == nk - 1)
        def _():
            o_ref[...] = acc_sc[...] / l_sc[...]
            lse_ref[...] = m_sc[...] * LN2 + jnp.log(l_sc[...])

    res = pl.pallas_call(
        body, name=name, grid=grid,
        in_specs=[pl.BlockSpec((bq, HEAD_DIM), lambda hh, i, j: (i, hh)),
                  pl.BlockSpec((bk, HEAD_DIM), lambda hh, i, j: (j, hh // grp)),
                  pl.BlockSpec((bk, HEAD_DIM), lambda hh, i, j: (j, hh // grp))] + rider.in_specs,
        out_specs=[pl.BlockSpec((bq, HEAD_DIM), lambda hh, i, j: (i, hh)),
                   pl.BlockSpec((None, bq, 1), lambda hh, i, j: (hh, i, 0))] + rider.out_specs,
        out_shape=[jax.ShapeDtypeStruct((tq, h * HEAD_DIM), F32), jax.ShapeDtypeStruct((h, tq, 1), F32)] + rider.out_shape,
        scratch_shapes=[pltpu.VMEM((bq, 1), F32), pltpu.VMEM((bq, 1), F32), pltpu.VMEM((bq, HEAD_DIM), F32),
                        pltpu.VMEM((bq, HEAD_DIM), BF16)] + rider.scratch,
        compiler_params=pltpu.CompilerParams(dimension_semantics=("arbitrary",) * 3))(q, k, v, *rider.arrays)
    return res[0], res[1], list(res[2:])


def _att_dq_call(name, q, k, v, o, lse, do, rider):
    tq, tk = q.shape[0], k.shape[0]
    h, kvh = q.shape[1] // HEAD_DIM, k.shape[1] // HEAD_DIM
    grp = h // kvh
    bq, bk = _att_tiles(tq, tk)
    nk = tk // bk
    scale = HEAD_DIM ** -0.5
    grid = (h, tq // bq, nk)
    nx = rider.n

    def body(*refs):
        q_ref, k_ref, v_ref, o_ref, lse_ref, do_ref = refs[:6]
        dq_ref = refs[6 + nx]
        delta_sc, lse2_sc, q2_sc = refs[7 + 2 * nx:10 + 2 * nx]
        rider.run(refs[6:6 + nx], refs[7 + nx:7 + 2 * nx], refs[10 + 2 * nx:], *_grid_ends(grid))
        kj = pl.program_id(2)

        @pl.when(kj == 0)
        def _():
            dq_ref[...] = jnp.zeros_like(dq_ref)
            delta_sc[...] = jnp.sum(do_ref[...] * o_ref[...], axis=1, keepdims=True)
            lse2_sc[...] = lse_ref[...] * LOG2E
            q2_sc[...] = (q_ref[...] * (scale * LOG2E)).astype(q2_sc.dtype)
        s = _dot(q2_sc[...], k_ref[...], ((1,), (1,)), False)
        p = jnp.exp2(s - lse2_sc[...])
        dp = _dot(do_ref[...], v_ref[...], ((1,), (1,)), False)
        ds = p * (dp - delta_sc[...])
        dq_ref[...] += _dot(ds, k_ref[...], ((1,), (0,)), False) * scale

    qspec = pl.BlockSpec((bq, HEAD_DIM), lambda hh, i, j: (i, hh))
    kspec = pl.BlockSpec((bk, HEAD_DIM), lambda hh, i, j: (j, hh // grp))
    res = pl.pallas_call(
        body, name=name, grid=grid,
        in_specs=[qspec, kspec, kspec, qspec, pl.BlockSpec((None, bq, 1), lambda hh, i, j: (hh, i, 0)), qspec] + rider.in_specs,
        out_specs=[qspec] + rider.out_specs, out_shape=[jax.ShapeDtypeStruct(q.shape, F32)] + rider.out_shape,
        scratch_shapes=[pltpu.VMEM((bq, 1), F32), pltpu.VMEM((bq, 1), F32), pltpu.VMEM((bq, HEAD_DIM), BF16)] + rider.scratch,
        compiler_params=pltpu.CompilerParams(dimension_semantics=("arbitrary",) * 3))(q, k, v, o, lse, do, *rider.arrays)
    return res[0], list(res[1:])


def _att_dkv_call(name, q, k, v, o, lse, do, rider):
    tq, tk = q.shape[0], k.shape[0]
    h, kvh = q.shape[1] // HEAD_DIM, k.shape[1] // HEAD_DIM
    grp = h // kvh
    bq, bk = _att_tiles(tq, tk)
    nq = tq // bq
    nr = grp * nq
    scale = HEAD_DIM ** -0.5
    grid = (kvh, tk // bk, nr)
    nx = rider.n

    def body(*refs):
        q_ref, k_ref, v_ref, o_ref, lse_ref, do_ref = refs[:6]
        dk_ref, dv_ref = refs[6 + nx:8 + nx]
        rider.run(refs[6:6 + nx], refs[8 + nx:8 + 2 * nx], refs[8 + 2 * nx:], *_grid_ends(grid))
        r = pl.program_id(2)

        @pl.when(r == 0)
        def _():
            dk_ref[...] = jnp.zeros_like(dk_ref)
            dv_ref[...] = jnp.zeros_like(dv_ref)
        q2 = _cast(q_ref[...] * (scale * LOG2E), False)
        s = _dot(q2, k_ref[...], ((1,), (1,)), False)
        p = jnp.exp2(s - lse_ref[...] * LOG2E)
        dv_ref[...] += _dot(p, do_ref[...], ((0,), (0,)), False)
        dp = _dot(do_ref[...], v_ref[...], ((1,), (1,)), False)
        delta = jnp.sum(do_ref[...] * o_ref[...], axis=1, keepdims=True)
        ds = p * (dp - delta)
        dk_ref[...] += _dot(ds, q2, ((0,), (0,)), False) * LN2

    qspec = pl.BlockSpec((bq, HEAD_DIM), lambda kh, j, r: (r % nq, kh * grp + r // nq))
    kspec = pl.BlockSpec((bk, HEAD_DIM), lambda kh, j, r: (j, kh))
    res = pl.pallas_call(
        body, name=name, grid=grid,
        in_specs=[qspec, kspec, kspec, qspec, pl.BlockSpec((None, bq, 1), lambda kh, j, r: (kh * grp + r // nq, r % nq, 0)), qspec]
                 + rider.in_specs,
        out_specs=[kspec, kspec] + rider.out_specs,
        out_shape=[jax.ShapeDtypeStruct(k.shape, F32), jax.ShapeDtypeStruct(v.shape, F32)] + rider.out_shape,
        scratch_shapes=rider.scratch,
        compiler_params=pltpu.CompilerParams(dimension_semantics=("arbitrary",) * 3))(q, k, v, o, lse, do, *rider.arrays)
    return res[0], res[1], list(res[2:])


def attention(name, q, k, v, shards=(), carry=()):
    shards, carry = list(shards), list(carry)
    half = (len(carry) + 1) // 2

    def fwd(q, k, v, shards, carry):
        o, lse, gathered = _att_fwd_call(name + "_fwd", q, k, v, _Rider(shards, 'gather'))
        return (o, gathered, carry), (q, k, v, o, lse, shards)

    def bwd(res, cts):
        q, k, v, o, lse, shards = res
        do, _, payload = cts
        dq, got_a = _att_dq_call(name + "_dq", q, k, v, o, lse, do, _Rider(payload[:half], 'a2a'))
        dk, dv, got_b = _att_dkv_call(name + "_dkv", q, k, v, o, lse, do, _Rider(payload[half:], 'a2a'))
        return dq, dk, dv, [jnp.zeros_like(a) for a in shards], got_a + got_b

    return _op(fwd, bwd)(q, k, v, shards, carry)


def loss_head(y, target):
    t, d = y.shape
    tm = min(ROW_TILE, t)

    def fwd(y, target):
        def body(y_ref, t_ref, l_ref, dy_ref):
            @pl.when(pl.program_id(0) == 0)
            def _():
                l_ref[...] = jnp.zeros_like(l_ref)
            err = y_ref[...] - t_ref[...]
            row = jnp.sum(err * err, axis=1, keepdims=True) * (1.0 / d)
            l_ref[...] += 0.5 * jnp.sum(row, axis=0, keepdims=True)
            dy_ref[...] = err * (1.0 / d)
        loss, dy = pl.pallas_call(
            body, name="loss_head", grid=(t // tm,),
            in_specs=[pl.BlockSpec((tm, d), lambda i: (i, 0)), pl.BlockSpec((tm, d), lambda i: (i, 0))],
            out_specs=[pl.BlockSpec((1, 1), lambda i: (0, 0)), pl.BlockSpec((tm, d), lambda i: (i, 0))],
            out_shape=[jax.ShapeDtypeStruct((1, 1), F32), jax.ShapeDtypeStruct((t, d), F32)],
            compiler_params=pltpu.CompilerParams(dimension_semantics=("arbitrary",)))(y, target)
        return loss[0, 0], (dy,)

    def bwd(res, g):
        return res[0] * g, jnp.zeros_like(res[0])

    return _op(fwd, bwd)(y, target)


def _my_index():
    return 4 * lax.axis_index("x") + 2 * lax.axis_index("y") + lax.axis_index("c")


def _xch_copies(ins, outs, modes, send_sems, recv_sems, local_sems):
    x, y, c = lax.axis_index("x"), lax.axis_index("y"), lax.axis_index("c")
    me = 4 * x + 2 * y + c
    copies = []
    for a in range(len(ins)):
        gather = modes[a] == 'gather'
        copies.append(pltpu.make_async_copy(ins[a] if gather else ins[a].at[me], outs[a].at[me], local_sems.at[a]))
        for dist in range(1, N_DEV):
            px = (1 - x) if dist & 4 else x
            py = (1 - y) if dist & 2 else y
            pc = (1 - c) if dist & 1 else c
            peer = 4 * px + 2 * py + pc
            copies.append(pltpu.make_async_remote_copy(
                src_ref=ins[a] if gather else ins[a].at[peer], dst_ref=outs[a].at[me],
                send_sem=send_sems.at[a * (N_DEV - 1) + dist - 1], recv_sem=recv_sems.at[a * (N_DEV - 1) + dist - 1],
                device_id=(px, py, pc), device_id_type=MESH))
    return copies


def _xch_out_shapes(arrays, modes):
    return [jax.ShapeDtypeStruct(((N_DEV,) + a.shape) if m == 'gather' else a.shape, a.dtype) for a, m in zip(arrays, modes)]


def _xch_sems(n):
    return [pltpu.SemaphoreType.DMA((n * (N_DEV - 1),)), pltpu.SemaphoreType.DMA((n * (N_DEV - 1),)), pltpu.SemaphoreType.DMA((n,))]


def exchange(name, arrays, modes):
    n = len(arrays)

    def body(*refs):
        copies = _xch_copies(refs[:n], refs[n:2 * n], modes, *refs[2 * n:])
        for cp in copies:
            cp.start()
        for cp in copies:
            cp.wait()

    hbm = pl.BlockSpec(memory_space=pl.ANY)
    return pl.pallas_call(
        body, name=name, in_specs=[hbm] * n, out_specs=[hbm] * n, out_shape=_xch_out_shapes(arrays, modes),
        scratch_shapes=_xch_sems(n), compiler_params=pltpu.CompilerParams(has_side_effects=True))(*arrays)


class _Rider:
    def __init__(self, arrays, mode):
        self.arrays, self.n, self.modes = list(arrays), len(arrays), [mode] * len(arrays)
        self.in_specs = [pl.BlockSpec(memory_space=pl.ANY)] * self.n
        self.out_specs = list(self.in_specs)
        self.out_shape = _xch_out_shapes(self.arrays, self.modes)
        self.scratch = _xch_sems(self.n) if self.n else []

    def run(self, ins, outs, sems, first, last):
        if not self.n:
            return

        @pl.when(first)
        def _():
            for cp in _xch_copies(ins, outs, self.modes, *sems):
                cp.start()

        @pl.when(last)
        def _():
            for cp in _xch_copies(ins, outs, self.modes, *sems):
                cp.wait()


def _adamw_math(w, g, m, v):
    m = ADAM_B1 * m + (1.0 - ADAM_B1) * g
    v = ADAM_B2 * v + (1.0 - ADAM_B2) * (g * g)
    m_hat = m / (1.0 - ADAM_B1 ** ADAM_STEP)
    v_hat = v / (1.0 - ADAM_B2 ** ADAM_STEP)
    delta = -ADAM_LR * (m_hat / (jnp.sqrt(v_hat) + ADAM_EPS) + ADAM_WD * w)
    return delta, m, v


def adamw(name, parts, w, m, v, scale=None, row_bytes_cap=1 << 20):
    p, r, c = parts.shape
    tr = _tile(r, max(8, row_bytes_cap // (4 * c) // 8 * 8), 8)
    extra = [] if scale is None else [scale]

    def body(p_ref, *refs):
        w_ref, m_ref, v_ref, g_ref, d_ref, nm_ref, nv_ref = refs[len(extra):]
        g = p_ref[0]
        for q in range(1, p):
            g = g + p_ref[q]
        if extra:
            g = g * refs[0][...]
        delta, nm, nv = _adamw_math(w_ref[...], g, m_ref[...], v_ref[...])
        g_ref[...] = g
        d_ref[...] = delta
        nm_ref[...] = nm
        nv_ref[...] = nv

    spec = pl.BlockSpec((tr, c), lambda i: (i, 0))
    shape = jax.ShapeDtypeStruct((r, c), F32)
    return pl.pallas_call(
        body, name=name, grid=(r // tr,),
        in_specs=[pl.BlockSpec((p, tr, c), lambda i: (0, i, 0))] + [spec] * (3 + len(extra)),
        out_specs=[spec] * 4, out_shape=[shape] * 4,
        compiler_params=pltpu.CompilerParams(dimension_semantics=("parallel",)))(parts, *extra, w, m, v)


def _rope_tables(n_lat, n_ctx):
    rows = n_lat // GRID_W
    row = jnp.repeat(jnp.arange(rows, dtype=F32), GRID_W)
    col = jnp.tile(jnp.arange(GRID_W, dtype=F32), rows)
    n_freq = HEAD_DIM // 4
    inv = ROPE_THETA ** (-jnp.arange(n_freq, dtype=F32) / n_freq)
    ang = jnp.concatenate([row[:, None] * inv, col[:, None] * inv], -1)
    cos, sin = jnp.cos(ang), jnp.sin(ang)
    cosf = jnp.concatenate([jnp.ones((n_ctx, HEAD_DIM), F32), jnp.concatenate([cos, cos], -1)], 0)
    sins = jnp.concatenate([jnp.zeros((n_ctx, HEAD_DIM), F32), jnp.concatenate([-sin, sin], -1)], 0)
    return cosf, sins


def _widths():
    rw, dw, aw, kw = RET_HEADS * HEAD_DIM, DN_HEADS * HEAD_DIM, ATT_HEADS * HEAD_DIM, ATT_KV_HEADS * HEAD_DIM
    return rw, dw, aw, kw


def _pad_in_proj(full):
    rw, dw, aw, kw = _widths()
    n_ab = 4 * DN_HEADS
    a0 = 4 * rw + 4 * dw
    main = jnp.concatenate([full[:, :a0], full[:, a0 + n_ab:]], 1)
    ab = jnp.concatenate([full[:, a0:a0 + n_ab], jnp.zeros((full.shape[0], LANE - n_ab), full.dtype)], 1)
    return jnp.concatenate([main, ab], 1)


def _unpad_in_proj(dpad):
    rw, dw, aw, kw = _widths()
    n_ab = 4 * DN_HEADS
    a0 = 4 * rw + 4 * dw
    main_w = dpad.shape[1] - LANE
    return jnp.concatenate([dpad[:, :a0], dpad[:, main_w:main_w + n_ab], dpad[:, a0:main_w]], 1)


def _chunked(a, chunk):
    t, g = a.shape
    at = a.T.reshape(g, t // chunk, chunk)
    return at[..., None], at[:, :, None, :]


def _mixer(h, wpad, like_in, sm, cosf, sins, n_ctx, tag, shards, carry):
    rw, dw, aw, kw = _widths()
    t = h.shape[0]
    nctx = n_ctx // min(ROW_TILE, t)
    rq, rk, rv, rg, dqkv, dz, aq, ak, av, ab = linear(f"in_proj{tag}", h, wpad, like_in,
                                                      split=[rw] * 4 + [3 * dw, dw, aw, kw, kw, LANE])
    big = _tile(t, 1024, 8)

    qr, kr = rowwise(f"ret_prep{tag}", _f_ret_prep, [rq, rk, cosf, sins], ['j', 'j', 'b', 'b'], [True, True, False, False],
                     [], [], [], [(HEAD_DIM, F32)] * 2, ncol=RET_HEADS, tm=big)
    logit = sm["ret_decay_logit"].reshape(2 * RET_HEADS, 1, 1)
    o_ret = chunk_scan(f"ret_scan{tag}", _f_retention, RET_CHUNK, RET_HEADS, n_ctx // RET_CHUNK, qr, kr, rv, [], [], [logit])
    y_ret, = rowwise(f"ret_out{tag}", _f_gated_out, [o_ret[0], o_ret[1], rg], ['j'] * 3, [True] * 3, [], [], [],
                     [(HEAD_DIM, F32)], ncol=RET_HEADS, tm=big)

    conv = short_conv(f"dn_conv{tag}", dqkv, sm["conv_w8"], nctx, [dw] * 3)
    dq, dk, dv = rowwise(f"dn_prep{tag}", _f_dn_prep, list(conv), ['j'] * 3, [True] * 3,
                         [], [], [], [(HEAD_DIM, F32)] * 3, ncol=DN_HEADS, tm=big)
    pad8 = lambda a: jnp.concatenate([a.reshape(1, 2 * DN_HEADS), jnp.zeros((1, LANE - 2 * DN_HEADS), F32)], 1)
    gb, = rowwise(f"dn_gates{tag}", _f_gates, [ab], ['j'], [True], [pad8(sm["dn_a_log"]), pad8(sm["dn_dt_bias"])],
                  ['shared'] * 2, [True] * 2, [(LANE, F32)], nctx=nctx)
    g_col, g_row = _chunked(gb[:, :2 * DN_HEADS], DN_CHUNK)
    b_col, _ = _chunked(gb[:, 2 * DN_HEADS:4 * DN_HEADS], DN_CHUNK)
    o_dn = chunk_scan(f"dn_scan{tag}", _f_delta, DN_CHUNK, DN_HEADS, n_ctx // DN_CHUNK, dq, dk, dv,
                      [a[:DN_HEADS] for a in (g_col, g_row, b_col)], [a[DN_HEADS:] for a in (g_col, g_row, b_col)], [])
    y_dn, = rowwise(f"dn_out{tag}", _f_gated_out_w, [o_dn[0], o_dn[1], dz], ['j'] * 3, [True] * 3,
                    [sm["dn_norm_w"].reshape(1, HEAD_DIM)], ['shared'], [True], [(HEAD_DIM, F32)], ncol=DN_HEADS, tm=big)

    qn, = rowwise(f"att_qn{tag}", _f_qk_norm, [aq, cosf, sins], ['j', 'b', 'b'], [True, False, False],
                  [sm["att_qn_w"].reshape(1, HEAD_DIM)], ['shared'], [True], [(HEAD_DIM, F32)], ncol=ATT_HEADS, tm=big)
    kn, = rowwise(f"att_kn{tag}", _f_qk_norm, [ak, cosf, sins], ['j', 'b', 'b'], [True, False, False],
                  [sm["att_kn_w"].reshape(1, HEAD_DIM)], ['shared'], [True], [(HEAD_DIM, F32)], ncol=ATT_KV_HEADS, tm=big)
    y_att_lat, gathered, carry = attention(f"att_lat{tag}", qn[n_ctx:], kn, av, shards, carry)
    return y_ret, y_dn, y_att_lat, (qn, kn, av), gathered, carry


def _layer(i, depth, n_ctx, cosf, sins, wt, shards, xs, mod, like, sm, carry):
    alpha = (2 * depth) ** 0.25
    d = xs.shape[1]
    last = i == depth - 1
    tag = f"_{i}"
    nctx = n_ctx // min(ROW_TILE, xs.shape[0])
    seg = lambda kk: mod[:, kk][:, None, :]
    lat = lambda kk: mod[1, kk][None, :]
    h, = rowwise(f"mod1{tag}", _f_mod, [xs], ['j'], [True], [seg(0), seg(1)], ['seg'] * 2, [True] * 2, [(d, F32)], nctx=nctx)
    y_ret, y_dn, y_att_lat, (qn, kn, av), gathered, carry = _mixer(h, wt["w_in"], like["w_in"], sm, cosf, sins, n_ctx, tag,
                                                                  shards, carry)
    vec = lambda a: a.reshape(1, d)
    ln1 = [vec(sm["ln1_w"]), vec(sm["ln1_b"])]
    ln2 = [vec(sm["ln2_w"]), vec(sm["ln2_b"])]
    if last:
        y = jnp.concatenate([y_ret[n_ctx:], y_dn[n_ctx:], y_att_lat], 1)
        xs = xs[n_ctx:]
        nctx = 0
        kinds = ['shared']
        g1, sh2, sc2, g2 = lat(2), lat(3), lat(4), lat(5)
    else:
        y_att_ctx, _, _ = attention(f"att_ctx{tag}", qn[:n_ctx], kn[:n_ctx], av[:n_ctx])
        y = jnp.concatenate([y_ret, y_dn, jnp.concatenate([y_att_ctx, y_att_lat], 0)], 1)
        kinds = ['seg']
        g1, sh2, sc2, g2 = seg(2), seg(3), seg(4), seg(5)
    tt = linear(f"out_proj{tag}", y, wt["w_o"], like["w_o"])
    x1, h2 = rowwise(f"norm1{tag}", functools.partial(_f_norm_mod, alpha), [xs, tt], ['j', 'j'], [True, True],
                     [g1] + ln1 + [sh2, sc2], kinds + ['shared'] * 2 + kinds * 2, [True] * 5, [(d, F32)] * 2, nctx=nctx)
    u = linear(f"ffn_in{tag}", h2, wt["w_ffn_in"], like["w_ffn_in"])
    act = swiglu(f"swiglu{tag}", u)
    t2 = linear(f"ffn_out{tag}", act, wt["w_ffn_out"], like["w_ffn_out"])
    xs, = rowwise(f"norm2{tag}", functools.partial(_f_norm, alpha), [x1, t2], ['j', 'j'], [True, True],
                  [g2] + ln2, kinds + ['shared'] * 2, [True] * 3, [(d, F32)], nctx=nctx)
    return (xs, carry), gathered


def _ada_forward(cond16, w_ada, b_loc):
    depth, d, n = w_ada.shape
    tn = _tile(n, 512)

    def body(c_ref, w_ref, b_ref, o_ref):
        o_ref[...] = jnp.dot(c_ref[...], w_ref[...], precision=HI, preferred_element_type=F32) + b_ref[...]

    return pl.pallas_call(
        body, name="ada_fwd", grid=(depth, n // tn),
        in_specs=[pl.BlockSpec((16, d), lambda l, j: (0, 0)), pl.BlockSpec((None, d, tn), lambda l, j: (l, 0, j)),
                  pl.BlockSpec((None, 1, tn), lambda l, j: (l, 0, j))],
        out_specs=pl.BlockSpec((None, 16, tn), lambda l, j: (l, 0, j)), out_shape=jax.ShapeDtypeStruct((depth, 16, n), F32),
        compiler_params=pltpu.CompilerParams(dimension_semantics=("parallel", "parallel")))(cond16, w_ada, b_loc)


def _ada_backward(cond16, w_ada, dm16):
    depth, d, n = w_ada.shape
    tn = _tile(n, 512)

    def body(c_ref, w_ref, dm_ref, gw_ref, dc_ref):
        @pl.when(jnp.logical_and(pl.program_id(0) == 0, pl.program_id(1) == 0))
        def _():
            dc_ref[...] = jnp.zeros_like(dc_ref)
        dm = dm_ref[...]
        gw_ref[...] = lax.dot_general(c_ref[...], dm, (((0,), (0,)), ((), ())), precision=HI, preferred_element_type=F32)
        dc_ref[...] += lax.dot_general(dm, w_ref[...], (((1,), (1,)), ((), ())), precision=HI, preferred_element_type=F32)

    return pl.pallas_call(
        body, name="ada_bwd", grid=(depth, n // tn),
        in_specs=[pl.BlockSpec((16, d), lambda l, j: (0, 0)), pl.BlockSpec((None, d, tn), lambda l, j: (l, 0, j)),
                  pl.BlockSpec((None, 16, tn), lambda l, j: (l, 0, j))],
        out_specs=[pl.BlockSpec((None, d, tn), lambda l, j: (l, 0, j)), pl.BlockSpec((16, d), lambda l, j: (0, 0))],
        out_shape=[jax.ShapeDtypeStruct((depth, d, n), F32), jax.ShapeDtypeStruct((16, d), F32)],
        compiler_params=pltpu.CompilerParams(dimension_semantics=("arbitrary", "arbitrary")))(cond16, w_ada, dm16)


def _silu_rows(c_all, c_ctx):
    d = c_ctx.shape[-1]
    stacked = jnp.concatenate([c_all.reshape(N_DEV, d), c_ctx.reshape(1, d), jnp.zeros((16 - N_DEV - 1, d), F32)], 0)

    def body(c_ref, o_ref, ds_ref):
        v = c_ref[...]
        row = lax.broadcasted_iota(jnp.int32, v.shape, 0)
        o_ref[...] = jnp.where(row <= N_DEV, _silu(v), 0.0)
        cc = c_ref[N_DEV:N_DEV + 1, :]
        sg = jax.nn.sigmoid(cc)
        ds_ref[...] = sg * (1.0 + cc * (1.0 - sg))

    return pl.pallas_call(body, name="cond_silu", out_shape=[jax.ShapeDtypeStruct((16, d), F32), jax.ShapeDtypeStruct((1, d), F32)])(stacked)


def _flat_pack(arrs):
    flat = jnp.concatenate([a.reshape(-1) for a in arrs])
    n = flat.shape[0]
    rows = -(-n // LANE)
    rows = -(-rows // 8) * 8
    return jnp.concatenate([flat, jnp.zeros((rows * LANE - n,), F32)]).reshape(rows, LANE)


def _flat_unpack(packed, shapes):
    flat = packed.reshape(-1)
    out, o = [], 0
    for s in shapes:
        n = int(np.prod(s))
        out.append(flat[o:o + n].reshape(s))
        o += n
    return out


XCH_ORDER = ["w_ffn_in", "w_o", "w_in", "w_ffn_out"]
SMALL = ["b_ada", "ret_decay_logit", "dn_a_log", "dn_dt_bias", "dn_norm_w", "att_qn_w", "att_kn_w", "ln1_w", "ln1_b", "ln2_w", "ln2_b"]
OUT_ORDER = ['c_ctx', 'w_ada', 'b_ada', 'w_in', 'ret_decay_logit', 'dn_conv_w', 'dn_a_log', 'dn_dt_bias', 'dn_norm_w', 'att_qn_w',
             'att_kn_w', 'w_o', 'ln1_w', 'ln1_b', 'w_ffn_in', 'w_ffn_out', 'ln2_w', 'ln2_b']


def kernel(x, c, ctx, c_ctx, w_ada, b_ada, w_in, ret_decay_logit, dn_conv_w, dn_a_log, dn_dt_bias, dn_norm_w, att_qn_w, att_kn_w, w_o, ln1_w, ln1_b, w_ffn_in, w_ffn_out, ln2_w, ln2_b, loss_target, m_c_ctx, m_w_ada, m_b_ada, m_w_in, m_ret_decay_logit, m_dn_conv_w, m_dn_a_log, m_dn_dt_bias, m_dn_norm_w, m_att_qn_w, m_att_kn_w, m_w_o, m_ln1_w, m_ln1_b, m_w_ffn_in, m_w_ffn_out, m_ln2_w, m_ln2_b, v_c_ctx, v_w_ada, v_b_ada, v_w_in, v_ret_decay_logit, v_dn_conv_w, v_dn_a_log, v_dn_dt_bias, v_dn_norm_w, v_att_qn_w, v_att_kn_w, v_w_o, v_ln1_w, v_ln1_b, v_w_ffn_in, v_w_ffn_out, v_ln2_w, v_ln2_b):
    weights = dict(c_ctx=c_ctx, w_ada=w_ada, b_ada=b_ada, w_in=w_in, ret_decay_logit=ret_decay_logit, dn_conv_w=dn_conv_w,
                   dn_a_log=dn_a_log, dn_dt_bias=dn_dt_bias, dn_norm_w=dn_norm_w, att_qn_w=att_qn_w, att_kn_w=att_kn_w, w_o=w_o,
                   ln1_w=ln1_w, ln1_b=ln1_b, w_ffn_in=w_ffn_in, w_ffn_out=w_ffn_out, ln2_w=ln2_w, ln2_b=ln2_b)
    mom1 = dict(c_ctx=m_c_ctx, w_ada=m_w_ada, b_ada=m_b_ada, w_in=m_w_in, ret_decay_logit=m_ret_decay_logit, dn_conv_w=m_dn_conv_w,
                dn_a_log=m_dn_a_log, dn_dt_bias=m_dn_dt_bias, dn_norm_w=m_dn_norm_w, att_qn_w=m_att_qn_w, att_kn_w=m_att_kn_w,
                w_o=m_w_o, ln1_w=m_ln1_w, ln1_b=m_ln1_b, w_ffn_in=m_w_ffn_in, w_ffn_out=m_w_ffn_out, ln2_w=m_ln2_w, ln2_b=m_ln2_b)
    mom2 = dict(c_ctx=v_c_ctx, w_ada=v_w_ada, b_ada=v_b_ada, w_in=v_w_in, ret_decay_logit=v_ret_decay_logit, dn_conv_w=v_dn_conv_w,
                dn_a_log=v_dn_a_log, dn_dt_bias=v_dn_dt_bias, dn_norm_w=v_dn_norm_w, att_qn_w=v_att_qn_w, att_kn_w=v_att_kn_w,
                w_o=v_w_o, ln1_w=v_ln1_w, ln1_b=v_ln1_b, w_ffn_in=v_w_ffn_in, w_ffn_out=v_w_ffn_out, ln2_w=v_ln2_w, ln2_b=v_ln2_b)
    depth, d, n_ada = w_ada.shape
    me = _my_index()
    x2, ctx2, tgt2 = x[0], ctx[0], loss_target[0]

    n_ctx, n_lat = ctx2.shape[0], x2.shape[0]
    proj_w = w_in.shape[2] * N_DEV
    big = dict(w_in=w_in.astype(BF16), w_o=w_o.astype(BF16), w_ffn_in=w_ffn_in.astype(BF16), w_ffn_out=w_ffn_out.astype(BF16))

    def shards_of(i):
        return [big[k][i] for k in XCH_ORDER]

    def assemble(gathered):
        g = dict(zip(XCH_ORDER, gathered))
        return dict(w_in=_pad_in_proj(jnp.moveaxis(g["w_in"], 0, 1).reshape(d, proj_w)), w_o=g["w_o"].reshape(d, d),
                    w_ffn_in=g["w_ffn_in"], w_ffn_out=g["w_ffn_out"].reshape(-1, d))

    def grad_slices(glike):
        g = dict(w_in=jnp.moveaxis(_unpad_in_proj(glike["w_in"]).reshape(d, N_DEV, -1), 1, 0), w_o=glike["w_o"].reshape(N_DEV, -1, d),
                 w_ffn_in=glike["w_ffn_in"], w_ffn_out=glike["w_ffn_out"].reshape(N_DEV, -1, d))
        return [g[k] for k in XCH_ORDER]

    first = exchange("gather_first", [c, dn_conv_w] + shards_of(0), ['gather'] * (2 + len(XCH_ORDER)))
    c_all, conv_all, gathered = first[0], first[1], first[2:]
    conv_full = jnp.moveaxis(conv_all, 0, 2).reshape(depth, DN_CONV_K, -1)
    conv_w8 = jnp.concatenate([conv_full, jnp.zeros((depth, 8 - DN_CONV_K, conv_full.shape[2]), F32)], 1)

    cond16, dsilu_ctx = _silu_rows(c_all, c_ctx)
    b_loc = lax.dynamic_slice_in_dim(b_ada, me * n_ada, n_ada, axis=1)[:, None, :]
    mod_loc = _ada_forward(cond16, w_ada, b_loc)
    mod_all, = exchange("gather_mods", [mod_loc], ['gather'])
    mod_full = jnp.moveaxis(mod_all, 0, 2).reshape(depth, 16, N_DEV * n_ada)
    mods = jnp.stack([mod_full[:, N_DEV], lax.dynamic_index_in_dim(mod_full, me, 1, keepdims=False)], 1).reshape(depth, 2, 6, d)

    sm_all = dict(ret_decay_logit=ret_decay_logit, dn_a_log=dn_a_log, dn_dt_bias=dn_dt_bias, dn_norm_w=dn_norm_w, att_qn_w=att_qn_w,
                  att_kn_w=att_kn_w, ln1_w=ln1_w, ln1_b=ln1_b, ln2_w=ln2_w, ln2_b=ln2_b, conv_w8=conv_w8)
    cosf, sins = _rope_tables(n_lat, n_ctx)
    xs = jnp.concatenate([ctx2, x2], 0)
    vjps = []
    for i in range(depth):
        wt = assemble(gathered)
        like = {k: jnp.zeros(a.shape, F32) for k, a in wt.items()}
        more = i + 1 < depth
        carry = [jnp.zeros(a.shape, F32) for a in jax.eval_shape(grad_slices, like)] if more else []
        layer = functools.partial(_layer, i, depth, n_ctx, cosf, sins, wt, shards_of(i + 1) if more else [])
        (xs, _), vjp_i, gathered = jax.vjp(layer, xs, mods[i], like, {k: v[i] for k, v in sm_all.items()}, carry, has_aux=True)
        vjps.append(vjp_i)
    loss, vjp_loss = jax.vjp(lambda y: loss_head(y, tgt2), xs)
    loss = lax.psum(loss, ("x", "y", "c"))

    g, = vjp_loss(jnp.ones((), F32))
    payload, received = [], [None] * depth
    gmods, gsm = [None] * depth, [None] * depth
    for i in reversed(range(depth)):
        g, gmods[i], glike, gsm[i], got = vjps[i]((g, payload))
        if i + 1 < depth:
            received[i + 1] = got
        payload = grad_slices(glike)
    gx = g[n_ctx:]

    gm = jnp.stack(gmods).reshape(depth, 2, 6 * d)
    small_parts = {k: jnp.stack([gs[k] for gs in gsm]) for k in gsm[0]}
    small_parts["b_ada"] = gm[:, 0] + gm[:, 1]
    small_pack = _flat_pack([small_parts[k] for k in SMALL])
    g_conv = jnp.moveaxis(small_parts["conv_w8"][:, :DN_CONV_K].reshape(depth, DN_CONV_K, N_DEV, -1), 2, 0)
    got = exchange("exchange_last", [gm, small_pack, g_conv] + payload, ['gather', 'gather'] + ['a2a'] * (1 + len(payload)))
    gm_all, small_all, conv_got = got[0], got[1], got[2]
    received[0] = got[3:]
    by_name = {k: jnp.stack([received[l][n] for l in range(depth)], 1) for n, k in enumerate(XCH_ORDER)}

    ctx_sum = gm_all[0, :, 0]
    for p in range(1, N_DEV):
        ctx_sum = ctx_sum + gm_all[p, :, 0]
    dm_full = jnp.concatenate([jnp.moveaxis(gm_all[:, :, 1], 0, 1), ctx_sum[:, None], jnp.zeros((depth, 16 - N_DEV - 1, 6 * d), F32)], 1)
    dm16 = lax.dynamic_slice_in_dim(dm_full, me * n_ada, n_ada, axis=2)
    g_w_ada, dcond = _ada_backward(cond16, w_ada, dm16)
    dcond_all, = exchange("gather_dcond", [dcond[N_DEV:N_DEV + 1]], ['gather'])

    out = {}

    def update(name, parts3, shape, scale=None):
        w2 = weights[name].reshape(parts3.shape[1:])
        g, dl, nm, nv = adamw("adamw_" + name, parts3, w2, mom1[name].reshape(w2.shape), mom2[name].reshape(w2.shape), scale)
        out[name] = tuple(a.reshape(shape) for a in (g, dl, nm, nv))

    update("w_in", by_name["w_in"].reshape(N_DEV, depth * d, -1), w_in.shape)
    update("w_o", by_name["w_o"].reshape(N_DEV, -1, d), w_o.shape)
    update("w_ffn_in", by_name["w_ffn_in"].reshape(N_DEV, depth * d, -1), w_ffn_in.shape)
    update("w_ffn_out", by_name["w_ffn_out"].reshape(N_DEV, -1, d), w_ffn_out.shape)
    update("dn_conv_w", conv_got.reshape(N_DEV, depth * DN_CONV_K, -1), dn_conv_w.shape)
    update("w_ada", g_w_ada.reshape(1, depth * d, n_ada), w_ada.shape)
    update("c_ctx", dcond_all.reshape(N_DEV, 1, d), c_ctx.shape, scale=dsilu_ctx)

    small_shapes = [weights[k].shape for k in SMALL]
    w_pack, m_pack, v_pack = (_flat_pack([src[k] for k in SMALL]) for src in (weights, mom1, mom2))
    packs = adamw("adamw_small", small_all, w_pack, m_pack, v_pack)
    for k, vals in zip(SMALL, zip(*[_flat_unpack(pk, small_shapes) for pk in packs])):
        out[k] = vals

    res = [loss, gx[None]]
    for slot in range(4):
        res += [out[k][slot] for k in OUT_ORDER]
    return tuple(res)
```

```python
import functools
import math

import jax
import jax.numpy as jnp
import numpy as np
from jax import lax
from jax.experimental import pallas as pl
from jax.experimental.pallas import tpu as pltpu

F32 = jnp.float32
BF16 = jnp.bfloat16
HI = lax.Precision.HIGHEST

HEAD_DIM = 128
RET_HEADS = 4
DN_HEADS = 4
ATT_HEADS = 8
ATT_KV_HEADS = 2
RET_CHUNK = 128
DN_CHUNK = 64
DN_CONV_K = 5
GRID_W = 64
ROPE_THETA = 10000.0
EPS = 1e-6
ADAM_LR = 0.001
ADAM_B1 = 0.9
ADAM_B2 = 0.999
ADAM_EPS = 1e-08
ADAM_WD = 0.01
ADAM_STEP = 10
N_DEV = 8
LANE = 128
LN2 = math.log(2.0)
LOG2E = 1.0 / LN2
ROW_TILE = 256
MESH = pl.DeviceIdType.MESH


def _tile(n, cap, unit=LANE):
    best = None
    for t in range(unit, min(n, cap) + 1, unit):
        if n % t == 0:
            best = t
    return n if best is None else best


def _op(fwd, bwd):
    @jax.custom_vjp
    def op(*args):
        return fwd(*args)[0]
    op.defvjp(fwd, bwd)
    return op


def _cast(x, exact):
    return x if exact else x.astype(BF16)


def _dot(a, b, dims, exact):
    return lax.dot_general(_cast(a, exact), _cast(b, exact), (dims, ((), ())),
                           precision=HI if exact else None, preferred_element_type=F32)


def mm_nn(name, a, b, exact=False, tm_cap=1024, tn_cap=1408, tk_cap=2048):
    m, k = a.shape
    slotted = b.ndim == 3
    if slotted:
        s, _, ns = b.shape
        n = s * ns
        tn = _tile(ns, tn_cap)
        per = ns // tn
    else:
        n = b.shape[1]
        tn = _tile(n, tn_cap)
    tm = _tile(m, tm_cap, 8)
    tk = _tile(k, tk_cap)
    nk = k // tk

    def body(a_ref, b_ref, o_ref):
        kk = pl.program_id(2)

        @pl.when(kk == 0)
        def _():
            o_ref[...] = jnp.zeros_like(o_ref)
        o_ref[...] += _dot(a_ref[...], b_ref[...], ((1,), (0,)), exact)

    if slotted:
        b_spec = pl.BlockSpec((None, tk, tn), lambda i, j, kk: (j // per, kk, j % per))
    else:
        b_spec = pl.BlockSpec((tk, tn), lambda i, j, kk: (kk, j))
    return pl.pallas_call(
        body, name=name, grid=(m // tm, n // tn, nk),
        in_specs=[pl.BlockSpec((tm, tk), lambda i, j, kk: (i, kk)), b_spec],
        out_specs=pl.BlockSpec((tm, tn), lambda i, j, kk: (i, j)),
        out_shape=jax.ShapeDtypeStruct((m, n), F32),
        compiler_params=pltpu.CompilerParams(dimension_semantics=("parallel", "parallel", "arbitrary")),
    )(a, b)


def mm_nt(name, a, b, exact=False, tm_cap=1024, tn_cap=2048, tk_cap=1408):
    m, n = a.shape
    slotted = b.ndim == 3
    if slotted:
        s, k, ns = b.shape
        tk = _tile(ns, tk_cap)
        per = ns // tk
    else:
        k = b.shape[0]
        tk = _tile(n, tk_cap)
    tm = _tile(m, tm_cap, 8)
    tn = _tile(k, tn_cap)
    nk = n // tk

    def body(a_ref, b_ref, o_ref):
        kk = pl.program_id(2)

        @pl.when(kk == 0)
        def _():
            o_ref[...] = jnp.zeros_like(o_ref)
        o_ref[...] += _dot(a_ref[...], b_ref[...], ((1,), (1,)), exact)

    if slotted:
        b_spec = pl.BlockSpec((None, tn, tk), lambda i, j, kk: (kk // per, j, kk % per))
    else:
        b_spec = pl.BlockSpec((tn, tk), lambda i, j, kk: (j, kk))
    return pl.pallas_call(
        body, name=name, grid=(m // tm, k // tn, nk),
        in_specs=[pl.BlockSpec((tm, tk), lambda i, j, kk: (i, kk)), b_spec],
        out_specs=pl.BlockSpec((tm, tn), lambda i, j, kk: (i, j)),
        out_shape=jax.ShapeDtypeStruct((m, k), F32),
        compiler_params=pltpu.CompilerParams(dimension_semantics=("parallel", "parallel", "arbitrary")),
    )(a, b)


def mm_tn(name, a, b, slots=None, exact=False, tm_cap=1024, tn_cap=1408, tk_cap=1024):
    m, k = a.shape
    n = b.shape[1]
    if slots is not None:
        s, ns = slots
        tn = _tile(ns, tn_cap)
        per = ns // tn
        out_shape = jax.ShapeDtypeStruct((s, k, ns), F32)
    else:
        tn = _tile(n, tn_cap)
        out_shape = jax.ShapeDtypeStruct((k, n), F32)
    tm = _tile(k, tm_cap)
    tk = _tile(m, tk_cap, 8)
    nk = m // tk

    def body(a_ref, b_ref, o_ref):
        kk = pl.program_id(2)

        @pl.when(kk == 0)
        def _():
            o_ref[...] = jnp.zeros_like(o_ref)
        o_ref[...] += _dot(a_ref[...], b_ref[...], ((0,), (0,)), exact)

    if slots is not None:
        o_spec = pl.BlockSpec((None, tm, tn), lambda i, j, kk: (j // per, i, j % per))
    else:
        o_spec = pl.BlockSpec((tm, tn), lambda i, j, kk: (i, j))
    return pl.pallas_call(
        body, name=name, grid=(k // tm, n // tn, nk),
        in_specs=[pl.BlockSpec((tk, tm), lambda i, j, kk: (kk, i)), pl.BlockSpec((tk, tn), lambda i, j, kk: (kk, j))],
        out_specs=o_spec, out_shape=out_shape,
        compiler_params=pltpu.CompilerParams(dimension_semantics=("parallel", "parallel", "arbitrary")),
    )(a, b)


def _split_cols(y, widths):
    offs = np.cumsum([0] + list(widths))
    return tuple(y[:, int(a):int(b)] for a, b in zip(offs[:-1], offs[1:]))


def linear(name, a, w, like, split=None):
    def fwd(a, w, like):
        y = mm_nn(name + "_fwd", a, w)
        return (y if split is None else _split_cols(y, split)), (a, w)

    def bwd(res, dy):
        a, w = res
        if split is not None:
            dy = jnp.concatenate(dy, axis=1)
        da = mm_nt(name + "_dx", dy, w)
        slots = (w.shape[0], w.shape[2]) if w.ndim == 3 else None
        dw = mm_tn(name + "_dw", a, dy, slots=slots)
        return da, jnp.zeros_like(w), dw

    return _op(fwd, bwd)(a, w, like)


def rowwise(name, f, rows, row_modes, row_diff, vecs, vec_kinds, vec_diff, out_defs, ncol=1, nctx=0, tm=None):
    nr, nv = len(rows), len(vecs)
    t = rows[0].shape[0]
    tm = min(ROW_TILE, t) if tm is None else tm
    assert t % tm == 0 and (tm == min(ROW_TILE, t) or 'seg' not in vec_kinds)
    nrow = t // tm

    def row_spec(a, mode):
        w = a.shape[1]
        if mode == 'j':
            return pl.BlockSpec((tm, w // ncol), lambda i, j: (i, j))
        return pl.BlockSpec((tm, w), lambda i, j: (i, 0))

    def vec_spec(a, kind):
        nd = a.ndim
        if kind == 'shared':
            return pl.BlockSpec(a.shape, lambda i, j: (0,) * nd)
        return pl.BlockSpec((None,) + a.shape[1:], lambda i, j: ((i >= nctx).astype(jnp.int32),) + (0,) * (nd - 1))

    in_specs = [row_spec(a, m) for a, m in zip(rows, row_modes)] + [vec_spec(a, k) for a, k in zip(vecs, vec_kinds)]
    out_specs = [pl.BlockSpec((tm, w), lambda i, j: (i, j)) for w, _ in out_defs]
    out_shape = [jax.ShapeDtypeStruct((t, ncol * w), dt) for w, dt in out_defs]
    params = pltpu.CompilerParams(dimension_semantics=("arbitrary", "arbitrary"))

    def fwd_call(*args):
        def body(*refs):
            vals = [r[...] for r in refs[:nr + nv]]
            outs = f(*vals)
            for o_ref, o in zip(refs[nr + nv:], outs):
                o_ref[...] = o.astype(o_ref.dtype)
        return pl.pallas_call(body, name=name + "_fwd", grid=(nrow, ncol), in_specs=in_specs, out_specs=out_specs,
                              out_shape=out_shape, compiler_params=params)(*args)

    diff_idx = [i for i in range(nr) if row_diff[i]] + [nr + i for i in range(nv) if vec_diff[i]]
    d_rows = [i for i in range(nr) if row_diff[i]]
    d_vecs = [i for i in range(nv) if vec_diff[i]]

    def bwd_call(args, cts):
        n_in = nr + nv + len(out_defs)

        def body(*refs):
            i, j = pl.program_id(0), pl.program_id(1)
            vals = [r[...] for r in refs[:nr + nv]]
            ct = tuple(r[...] for r in refs[nr + nv:n_in])

            def g(*dvals):
                full = list(vals)
                for idx, v in zip(diff_idx, dvals):
                    full[idx] = v
                return tuple(f(*full))

            outs, vjp = jax.vjp(g, *[vals[idx] for idx in diff_idx])
            grads = vjp(tuple(c.astype(o.dtype) for c, o in zip(ct, outs)))
            out_refs = refs[n_in:]
            for p, _ in enumerate(d_rows):
                out_refs[p][...] = grads[p].astype(out_refs[p].dtype)
            for p, vi in enumerate(d_vecs):
                ref = out_refs[len(d_rows) + p]
                if vec_kinds[vi] == 'shared':
                    first = jnp.logical_and(i == 0, j == 0)
                else:
                    first = jnp.logical_and(jnp.logical_or(i == 0, i == nctx), j == 0)

                @pl.when(first)
                def _():
                    ref[...] = jnp.zeros_like(ref)
                ref[...] += grads[len(d_rows) + p].astype(F32)

        ct_specs = [pl.BlockSpec((tm, w), lambda i, j: (i, j)) for w, _ in out_defs]
        g_specs = [row_spec(rows[i], row_modes[i]) for i in d_rows] + [vec_spec(vecs[i], vec_kinds[i]) for i in d_vecs]
        g_shape = [jax.ShapeDtypeStruct(rows[i].shape, rows[i].dtype) for i in d_rows] + \
                  [jax.ShapeDtypeStruct(vecs[i].shape, F32) for i in d_vecs]
        return pl.pallas_call(body, name=name + "_bwd", grid=(nrow, ncol), in_specs=in_specs + ct_specs, out_specs=g_specs,
                              out_shape=g_shape, compiler_params=params)(*args, *cts)

    def fwd(*args):
        return tuple(fwd_call(*args)), args

    def bwd(args, cts):
        grads = bwd_call(args, cts)
        out = [None] * (nr + nv)
        for p, idx in enumerate(diff_idx):
            out[idx] = grads[p]
        for idx in range(nr + nv):
            if out[idx] is None:
                out[idx] = jnp.zeros_like(args[idx])
        return tuple(out)

    for i in d_rows:
        assert row_modes[i] == 'j' or ncol == 1
    return _op(fwd, bwd)(*rows, *vecs)


def _silu(x):
    return x * jax.nn.sigmoid(x)


def _roll_half(x):
    return pltpu.roll(x, HEAD_DIM // 2, axis=1)


@jax.custom_vjp
def _rope(x, cosf, sins):
    return x * cosf + _roll_half(x) * sins


def _rope_fwd(x, cosf, sins):
    return _rope(x, cosf, sins), (cosf, sins)


def _rope_bwd(res, dy):
    cosf, sins = res
    return dy * cosf + _roll_half(dy * sins), jnp.zeros_like(cosf), jnp.zeros_like(sins)


_rope.defvjp(_rope_fwd, _rope_bwd)


def _f_mod(x, shift, scale):
    return (x * (1.0 + scale) + shift,)


def _layer_norm(z, w, b):
    mu = jnp.mean(z, -1, keepdims=True)
    zc = z - mu
    var = jnp.mean(zc * zc, -1, keepdims=True)
    return zc * lax.rsqrt(var + EPS) * w + b


def _f_norm_mod(alpha, x, t, gate, w, b, shift, scale):
    xn = _layer_norm(alpha * x + gate * t, w, b)
    return xn, xn * (1.0 + scale) + shift


def _f_norm(alpha, x, t, gate, w, b):
    return (_layer_norm(alpha * x + gate * t, w, b),)


def _f_ret_prep(q, k, cosf, sins):
    return _rope(q, cosf, sins), _rope(k, cosf, sins) * HEAD_DIM ** -0.5


def _rms(x):
    return x * lax.rsqrt(jnp.mean(x * x, -1, keepdims=True) + EPS)


def _f_gated_out(of, ob, gate):
    return (_rms(of + ob) * _silu(gate),)


def _f_gated_out_w(of, ob, gate, w):
    return (_rms(of + ob) * w * _silu(gate),)


def _l2n(x):
    return x * lax.rsqrt(jnp.sum(x * x, -1, keepdims=True) + EPS)


def _f_dn_prep(cq, ck, cv):
    return _l2n(_silu(cq)) * HEAD_DIM ** -0.5, _l2n(_silu(ck)), _silu(cv)


def _f_qk_norm(x, cosf, sins, w):
    return (_rope(_rms(x) * w, cosf, sins),)


def _f_gates(ab, alog, dtb):
    tm = ab.shape[0]
    lane = lax.broadcasted_iota(jnp.int32, (1, LANE), 1)
    g = -jnp.exp(alog) * jax.nn.softplus(ab + dtb)
    beta = jax.nn.sigmoid(ab)
    r = lax.broadcasted_iota(jnp.int32, (tm, tm), 0)
    c = lax.broadcasted_iota(jnp.int32, (tm, tm), 1)
    same = (r // DN_CHUNK) == (c // DN_CHUNK)
    lower = jnp.where(jnp.logical_and(same, c <= r), 1.0, 0.0).astype(F32)
    upper = jnp.where(jnp.logical_and(same, c >= r), 1.0, 0.0).astype(F32)
    gl = jnp.dot(lower, g, precision=HI, preferred_element_type=F32)
    gu = jnp.dot(upper, g, precision=HI, preferred_element_type=F32)
    out = jnp.where(lane < DN_HEADS, gl, jnp.where(lane < 2 * DN_HEADS, gu, jnp.where(lane < 4 * DN_HEADS, beta, 0.0)))
    return (out,)


def swiglu(name, u, tn_cap=1408):
    t, f2 = u.shape
    ff = f2 // 2
    tn = _tile(ff, tn_cap)
    nc = ff // tn
    tm = min(ROW_TILE, t)

    def fwd_call(u):
        def body(g_ref, u_ref, o_ref):
            o_ref[...] = _silu(g_ref[...]) * u_ref[...]
        return pl.pallas_call(
            body, name=name + "_fwd", grid=(t // tm, nc),
            in_specs=[pl.BlockSpec((tm, tn), lambda i, j: (i, j)), pl.BlockSpec((tm, tn), lambda i, j: (i, j + nc))],
            out_specs=pl.BlockSpec((tm, tn), lambda i, j: (i, j)), out_shape=jax.ShapeDtypeStruct((t, ff), F32),
            compiler_params=pltpu.CompilerParams(dimension_semantics=("parallel", "parallel")))(u, u)

    def bwd_call(u, da):
        tb = min(64, t)

        def body(u_ref, da_ref, o_ref):
            gate, up, d = u_ref[:, :ff], u_ref[:, ff:], da_ref[...]
            sg = jax.nn.sigmoid(gate)
            o_ref[:, :ff] = d * up * (sg * (1.0 + gate * (1.0 - sg)))
            o_ref[:, ff:] = d * (gate * sg)
        return pl.pallas_call(
            body, name=name + "_bwd", grid=(t // tb,),
            in_specs=[pl.BlockSpec((tb, f2), lambda i: (i, 0)), pl.BlockSpec((tb, ff), lambda i: (i, 0))],
            out_specs=pl.BlockSpec((tb, f2), lambda i: (i, 0)), out_shape=jax.ShapeDtypeStruct((t, f2), F32),
            compiler_params=pltpu.CompilerParams(dimension_semantics=("parallel",)))(u, da)

    return _op(lambda u: (fwd_call(u), (u,)), lambda res, da: (bwd_call(res[0], da),))(u)


HALO = 8


def _conv_specs(t, c, tm, tc):
    nb8 = t // HALO
    per = tm // HALO
    cur = pl.BlockSpec((tm, tc), lambda j, i: (i, j))
    prev = pl.BlockSpec((HALO, tc), lambda j, i: (jnp.maximum(i * per - 1, 0), j))
    nxt = pl.BlockSpec((HALO, tc), lambda j, i: (jnp.minimum((i + 1) * per, nb8 - 1), j))
    return cur, prev, nxt


def _extended(prev_ref, cur_ref, next_ref, ext_ref, i, nrow, nctx, tm):
    has_prev = jnp.logical_and(i != 0, i != nctx)
    has_next = jnp.logical_and(i != nrow - 1, i != nctx - 1)
    ext_ref[0:HALO, :] = jnp.where(has_prev, prev_ref[...], 0.0)
    ext_ref[HALO:HALO + tm, :] = cur_ref[...]
    ext_ref[HALO + tm:, :] = jnp.where(has_next, next_ref[...], 0.0)


def _conv_call(name, x, w8, nctx, flip):
    t, c = x.shape
    tm = min(ROW_TILE, t)
    tc = _tile(c, 512)
    nrow = t // tm
    pad = DN_CONV_K // 2

    def body(cur_ref, prev_ref, next_ref, w_ref, o_ref, ext_ref):
        i = pl.program_id(1)
        _extended(prev_ref, cur_ref, next_ref, ext_ref, i, nrow, nctx, tm)
        acc = jnp.zeros((tm, tc), F32)
        for j in range(DN_CONV_K):
            wj = w_ref[(DN_CONV_K - 1 - j) if flip else j, :][None, :]
            acc = acc + wj * ext_ref[HALO - pad + j:HALO - pad + j + tm, :]
        o_ref[...] = acc

    cur, prev, nxt = _conv_specs(t, c, tm, tc)
    return pl.pallas_call(
        body, name=name, grid=(c // tc, nrow),
        in_specs=[cur, prev, nxt, pl.BlockSpec((8, tc), lambda j, i: (0, j))],
        out_specs=pl.BlockSpec((tm, tc), lambda j, i: (i, j)), out_shape=jax.ShapeDtypeStruct((t, c), F32),
        scratch_shapes=[pltpu.VMEM((tm + 2 * HALO, tc), F32)],
        compiler_params=pltpu.CompilerParams(dimension_semantics=("arbitrary", "arbitrary")))(x, x, x, w8)


def _conv_dw_call(name, x, dy, nctx):
    t, c = x.shape
    tm = min(ROW_TILE, t)
    tc = _tile(c, 512)
    nrow = t // tm
    pad = DN_CONV_K // 2

    def body(cur_ref, prev_ref, next_ref, dy_ref, o_ref, ext_ref):
        i = pl.program_id(1)
        _extended(prev_ref, cur_ref, next_ref, ext_ref, i, nrow, nctx, tm)

        @pl.when(i == 0)
        def _():
            o_ref[...] = jnp.zeros_like(o_ref)
        dy = dy_ref[...]
        rows = [jnp.sum(dy * ext_ref[HALO - pad + j:HALO - pad + j + tm, :], axis=0, keepdims=True) for j in range(DN_CONV_K)]
        rows += [jnp.zeros((1, tc), F32)] * (8 - DN_CONV_K)
        o_ref[...] += jnp.concatenate(rows, axis=0)

    cur, prev, nxt = _conv_specs(t, c, tm, tc)
    return pl.pallas_call(
        body, name=name, grid=(c // tc, nrow),
        in_specs=[cur, prev, nxt, pl.BlockSpec((tm, tc), lambda j, i: (i, j))],
        out_specs=pl.BlockSpec((8, tc), lambda j, i: (0, j)), out_shape=jax.ShapeDtypeStruct((8, c), F32),
        scratch_shapes=[pltpu.VMEM((tm + 2 * HALO, tc), F32)],
        compiler_params=pltpu.CompilerParams(dimension_semantics=("arbitrary", "arbitrary")))(x, x, x, dy)


def short_conv(name, x, w8, nctx, split):
    def fwd(x, w8):
        return _split_cols(_conv_call(name + "_fwd", x, w8, nctx, False), split), (x, w8)

    def bwd(res, dy):
        x, w8 = res
        dy = jnp.concatenate(dy, axis=1)
        return _conv_call(name + "_dx", dy, w8, nctx, True), _conv_dw_call(name + "_dw", x, dy, nctx)

    return _op(fwd, bwd)(x, w8)


def _split_bf16(x):
    hi = x.astype(BF16)
    return hi, (x - hi.astype(F32)).astype(BF16)


def _mm_raw(a, b, form, exact):
    dims = {"nn": ((2,), (1,)), "nt": ((2,), (2,)), "tn": ((1,), (1,))}[form]
    dg = lambda p, q: lax.dot_general(p, q, (dims, ((0,), (0,))), preferred_element_type=F32)
    if not exact:
        return dg(_cast(a, False), _cast(b, False))
    ah, al = _split_bf16(a)
    bh, bl = _split_bf16(b)
    return dg(ah, bh) + (dg(ah, bl) + dg(al, bh))


@functools.partial(jax.custom_vjp, nondiff_argnums=(2, 3))
def _mm(a, b, form, exact=False):
    return _mm_raw(a, b, form, exact)


def _mm_fwd(a, b, form, exact):
    return _mm_raw(a, b, form, exact), (a, b)


def _mm_bwd(form, exact, res, dc):
    a, b = res
    if form == "nn":
        return _mm_raw(dc, b, "nt", exact), _mm_raw(a, dc, "tn", exact)
    if form == "nt":
        return _mm_raw(dc, b, "nn", exact), _mm_raw(dc, a, "tn", exact)
    return _mm_raw(b, dc, "nt", exact), _mm_raw(a, dc, "nn", exact)


_mm.defvjp(_mm_fwd, _mm_bwd)


def chunk_scan(name, f, chunk, heads, ncc, q, k, v, gargs_f, gargs_b, pargs):
    t = q.shape[0]
    nc = t // chunk
    g_all = 2 * heads
    d = HEAD_DIM
    hw = heads * d
    ng, npar = len(gargs_f), len(pargs)

    def pos_b(n):
        return jnp.where(n < ncc, ncc - 1 - n, ncc + nc - 1 - n)

    def specs(rev_visit):
        def vis(n):
            return (nc - 1 - n) if rev_visit else n
        qf = pl.BlockSpec((chunk, hw), lambda n: (vis(n), 0))
        qb = pl.BlockSpec((chunk, hw), lambda n: (pos_b(vis(n)), 0))
        gf = [pl.BlockSpec((heads, None) + a.shape[2:], lambda n: (0, vis(n), 0, 0)) for a in gargs_f]
        gb = [pl.BlockSpec((heads, None) + a.shape[2:], lambda n: (0, pos_b(vis(n)), 0, 0)) for a in gargs_b]
        ps = [pl.BlockSpec(a.shape, lambda n: (0, 0, 0)) for a in pargs]
        ss = pl.BlockSpec((g_all, None, d, d), lambda n: (0, vis(n), 0, 0))
        return qf, qb, gf, gb, ps, ss

    params = pltpu.CompilerParams(dimension_semantics=("arbitrary",))
    n_in = 6 + 2 * ng + npar

    def cols(h):
        return slice(h * d, (h + 1) * d)

    def stacked_inputs(refs):
        qkv = [jnp.stack([refs[i][:, cols(h)] for h in range(heads)] + [refs[3 + i][:, cols(h)] for h in range(heads)])
               for i in range(3)]
        gar = [jnp.concatenate([refs[6 + i][...], refs[6 + ng + i][...]], axis=0) for i in range(ng)]
        par = [r[...] for r in refs[6 + 2 * ng:n_in]]
        return qkv + gar + par

    def scatter(vals, refs_f, refs_b):
        for r_f, r_b, val in zip(refs_f, refs_b, vals):
            for h in range(heads):
                r_f[:, cols(h)] = val[h]
                r_b[:, cols(h)] = val[heads + h]

    def fwd_call(q, k, v, *rest):
        qf, qb, gf, gb, ps, ss = specs(False)

        def body(*refs):
            of_ref, ob_ref, sp_ref, s_ref = refs[n_in:]

            @pl.when(pl.program_id(0) == 0)
            def _():
                s_ref[...] = jnp.zeros_like(s_ref)
            s_prev = s_ref[...]
            sp_ref[...] = s_prev
            o, s_new = f(heads, *stacked_inputs(refs), s_prev)
            scatter([o], [of_ref], [ob_ref])
            s_ref[...] = s_new

        oshape = jax.ShapeDtypeStruct((t, hw), F32)
        return pl.pallas_call(
            body, name=name + "_fwd", grid=(nc,), in_specs=[qf] * 3 + [qb] * 3 + gf + gb + ps, out_specs=[qf, qb, ss],
            out_shape=[oshape, oshape, jax.ShapeDtypeStruct((g_all, nc, d, d), F32)],
            scratch_shapes=[pltpu.VMEM((g_all, d, d), F32)], compiler_params=params)(q, k, v, q, k, v, *rest)

    def bwd_call(q, k, v, rest, s_prev, do_f, do_b):
        qf, qb, gf, gb, ps, ss = specs(True)

        def body(*refs):
            sp_ref, dof_ref, dob_ref = refs[n_in:n_in + 3]
            outs = refs[n_in + 3:]
            dqkv_f, dqkv_b = outs[0:3], outs[3:6]
            dg_f, dg_b = outs[6:6 + ng], outs[6 + ng:6 + 2 * ng]
            dp_refs = outs[6 + 2 * ng:6 + 2 * ng + npar]
            ds_ref = outs[6 + 2 * ng + npar]

            @pl.when(pl.program_id(0) == 0)
            def _():
                ds_ref[...] = jnp.zeros_like(ds_ref)
                for r in dp_refs:
                    r[...] = jnp.zeros_like(r)
            _, vjp = jax.vjp(functools.partial(f, heads), *stacked_inputs(refs), sp_ref[...])
            do = jnp.stack([dof_ref[:, cols(h)] for h in range(heads)] + [dob_ref[:, cols(h)] for h in range(heads)])
            grads = vjp((do, ds_ref[...]))
            scatter(grads[:3], dqkv_f, dqkv_b)
            for r_f, r_b, gr in zip(dg_f, dg_b, grads[3:3 + ng]):
                r_f[...] = gr[:heads]
                r_b[...] = gr[heads:]
            for r, gr in zip(dp_refs, grads[3 + ng:3 + ng + npar]):
                r[...] += gr
            ds_ref[...] = grads[3 + ng + npar]

        dshape = jax.ShapeDtypeStruct((t, hw), F32)
        return pl.pallas_call(
            body, name=name + "_bwd", grid=(nc,), in_specs=[qf] * 3 + [qb] * 3 + gf + gb + ps + [ss, qf, qb],
            out_specs=[qf] * 3 + [qb] * 3 + gf + gb + ps,
            out_shape=[dshape] * 6 + [jax.ShapeDtypeStruct(a.shape, F32) for a in list(gargs_f) + list(gargs_b) + list(pargs)],
            scratch_shapes=[pltpu.VMEM((g_all, d, d), F32)], compiler_params=params)(q, k, v, q, k, v, *rest, s_prev, do_f, do_b)

    def fwd(q, k, v, *rest):
        o_f, o_b, s_prev = fwd_call(q, k, v, *rest)
        return (o_f, o_b), (q, k, v, rest, s_prev)

    def bwd(res, do):
        q, k, v, rest, s_prev = res
        grads = bwd_call(q, k, v, rest, s_prev, do[0], do[1])
        return tuple(grads[i] + grads[3 + i] for i in range(3)) + tuple(grads[6:])

    return _op(fwd, bwd)(q, k, v, *gargs_f, *gargs_b, *pargs)


def _order_masks(heads, c):
    g = 2 * heads
    rev = lax.broadcasted_iota(jnp.int32, (g, c, c), 0) >= heads
    i = lax.broadcasted_iota(jnp.int32, (g, c, c), 1)
    j = lax.broadcasted_iota(jnp.int32, (g, c, c), 2)
    rel = jnp.where(rev, j - i, i - j)
    pos = lax.broadcasted_iota(jnp.int32, (g, c, 1), 1)
    p = jnp.where(lax.broadcasted_iota(jnp.int32, (g, c, 1), 0) >= heads, c - 1 - pos, pos)
    return rel, p


def _f_retention(heads, q, k, v, logit, s):
    c = RET_CHUNK
    lg = jax.nn.log_sigmoid(logit)
    rel, p = _order_masks(heads, c)
    rel = rel.astype(F32)
    p = p.astype(F32)
    decay = jnp.where(rel >= 0, jnp.exp(jnp.maximum(rel, 0.0) * lg), 0.0)
    o_intra = _mm(_mm(q, k, "nt") * decay, v, "nn")
    q_decay = jnp.exp((p + 1.0) * lg)
    k_decay = jnp.exp((c - 1.0 - p) * lg)
    o_inter = _mm(q * q_decay, s, "nn")
    s_new = s * jnp.exp(c * lg) + _mm(k * k_decay, v, "tn")
    return o_intra + o_inter, s_new


def _f_delta(heads, q, k, v, gc, gr, bc, s):
    c = DN_CHUNK
    rel, p = _order_masks(heads, c)
    tri = rel >= 0
    strict = rel > 0
    decay = jnp.where(tri, jnp.exp(jnp.where(tri, gc - gr, 0.0)), 0.0)
    kb = k * bc
    vb = v * bc
    a = jnp.where(strict, _mm(kb, k, "nt") * decay, 0.0)
    pw = -a
    tinv = jnp.where(rel == 0, 1.0, 0.0).astype(F32) + pw
    for _ in range(int(math.log2(c)) - 1):
        pw = _mm(pw, pw, "nn", True)
        tinv = tinv + _mm(tinv, pw, "nn", True)
    eg = jnp.exp(gc)
    w_val = _mm(tinv, vb, "nn")
    k_cum = _mm(tinv, kb * eg, "nn")
    qk = _mm(q, k, "nt") * decay
    g_last = jnp.sum(jnp.where(p == c - 1, gc, 0.0), axis=1, keepdims=True)
    k_g = k * jnp.exp(g_last - gc)
    v_new = w_val - _mm(k_cum, s, "nn")
    o = _mm(q * eg, s, "nn") + _mm(qk, v_new, "nn")
    s_new = s * jnp.exp(g_last) + _mm(k_g, v_new, "tn")
    return o, s_new


def _att_tiles(tq, tk):
    return _tile(tq, 1024, 8), _tile(tk, 1408, 8)


def _grid_ends(grid):
    ids = [pl.program_id(a) for a in range(len(grid))]
    first = functools.reduce(jnp.logical_and, [i == 0 for i in ids])
    last = functools.reduce(jnp.logical_and, [i == n - 1 for i, n in zip(ids, grid)])
    return first, last


def _att_fwd_call(name, q, k, v, rider):
    tq, tk = q.shape[0], k.shape[0]
    h, kvh = q.shape[1] // HEAD_DIM, k.shape[1] // HEAD_DIM
    grp = h // kvh
    bq, bk = _att_tiles(tq, tk)
    nk = tk // bk
    scale = HEAD_DIM ** -0.5
    grid = (h, tq // bq, nk)
    nx = rider.n

    def body(*refs):
        q_ref, k_ref, v_ref = refs[:3]
        o_ref, lse_ref = refs[3 + nx:5 + nx]
        m_sc, l_sc, acc_sc, q2_sc = refs[5 + 2 * nx:9 + 2 * nx]
        rider.run(refs[3:3 + nx], refs[5 + nx:5 + 2 * nx], refs[9 + 2 * nx:], *_grid_ends(grid))
        kj = pl.program_id(2)

        @pl.when(kj == 0)
        def _():
            m_sc[...] = jnp.full_like(m_sc, -jnp.inf)
            l_sc[...] = jnp.zeros_like(l_sc)
            acc_sc[...] = jnp.zeros_like(acc_sc)
            q2_sc[...] = (q_ref[...] * (scale * LOG2E)).astype(q2_sc.dtype)
        s = _dot(q2_sc[...], k_ref[...], ((1,), (1,)), False)
        m_prev = m_sc[...]
        m_new = jnp.maximum(m_prev, jnp.max(s, axis=1, keepdims=True))
        p = jnp.exp2(s - m_new)
        alpha = jnp.exp2(m_prev - m_new)
        l_sc[...] = alpha * l_sc[...] + jnp.sum(p, axis=1, keepdims=True)
        acc_sc[...] = alpha * acc_sc[...] + _dot(p, v_ref[...], ((1,), (0,)), False)
        m_sc[...] = m_new

        @pl.when(kj == nk - 1)
        def _():
            o_ref[...] = acc_sc[...] / l_sc[...]
            lse_ref[...] = m_sc[...] * LN2 + jnp.log(l_sc[...])

    res = pl.pallas_call(
        body, name=name, grid=grid,
        in_specs=[pl.BlockSpec((bq, HEAD_DIM), lambda hh, i, j: (i, hh)),
                  pl.BlockSpec((bk, HEAD_DIM), lambda hh, i, j: (j, hh // grp)),
                  pl.BlockSpec((bk, HEAD_DIM), lambda hh, i, j: (j, hh // grp))] + rider.in_specs,
        out_specs=[pl.BlockSpec((bq, HEAD_DIM), lambda hh, i, j: (i, hh)),
                   pl.BlockSpec((None, bq, 1), lambda hh, i, j: (hh, i, 0))] + rider.out_specs,
        out_shape=[jax.ShapeDtypeStruct((tq, h * HEAD_DIM), F32), jax.ShapeDtypeStruct((h, tq, 1), F32)] + rider.out_shape,
        scratch_shapes=[pltpu.VMEM((bq, 1), F32), pltpu.VMEM((bq, 1), F32), pltpu.VMEM((bq, HEAD_DIM), F32),
                        pltpu.VMEM((bq, HEAD_DIM), BF16)] + rider.scratch,
        compiler_params=pltpu.CompilerParams(dimension_semantics=("arbitrary",) * 3))(q, k, v, *rider.arrays)
    return res[0], res[1], list(res[2:])


def _att_dq_call(name, q, k, v, o, lse, do, rider):
    tq, tk = q.shape[0], k.shape[0]
    h, kvh = q.shape[1] // HEAD_DIM, k.shape[1] // HEAD_DIM
    grp = h // kvh
    bq, bk = _att_tiles(tq, tk)
    nk = tk // bk
    scale = HEAD_DIM ** -0.5
    grid = (h, tq // bq, nk)
    nx = rider.n

    def body(*refs):
        q_ref, k_ref, v_ref, o_ref, lse_ref, do_ref = refs[:6]
        dq_ref = refs[6 + nx]
        delta_sc, lse2_sc, q2_sc = refs[7 + 2 * nx:10 + 2 * nx]
        rider.run(refs[6:6 + nx], refs[7 + nx:7 + 2 * nx], refs[10 + 2 * nx:], *_grid_ends(grid))
        kj = pl.program_id(2)

        @pl.when(kj == 0)
        def _():
            dq_ref[...] = jnp.zeros_like(dq_ref)
            delta_sc[...] = jnp.sum(do_ref[...] * o_ref[...], axis=1, keepdims=True)
            lse2_sc[...] = lse_ref[...] * LOG2E
            q2_sc[...] = (q_ref[...] * (scale * LOG2E)).astype(q2_sc.dtype)
        s = _dot(q2_sc[...], k_ref[...], ((1,), (1,)), False)
        p = jnp.exp2(s - lse2_sc[...])
        dp = _dot(do_ref[...], v_ref[...], ((1,), (1,)), False)
        ds = p * (dp - delta_sc[...])
        dq_ref[...] += _dot(ds, k_ref[...], ((1,), (0,)), False) * scale

    qspec = pl.BlockSpec((bq, HEAD_DIM), lambda hh, i, j: (i, hh))
    kspec = pl.BlockSpec((bk, HEAD_DIM), lambda hh, i, j: (j, hh // grp))
    res = pl.pallas_call(
        body, name=name, grid=grid,
        in_specs=[qspec, kspec, kspec, qspec, pl.BlockSpec((None, bq, 1), lambda hh, i, j: (hh, i, 0)), qspec] + rider.in_specs,
        out_specs=[qspec] + rider.out_specs, out_shape=[jax.ShapeDtypeStruct(q.shape, F32)] + rider.out_shape,
        scratch_shapes=[pltpu.VMEM((bq, 1), F32), pltpu.VMEM((bq, 1), F32), pltpu.VMEM((bq, HEAD_DIM), BF16)] + rider.scratch,
        compiler_params=pltpu.CompilerParams(dimension_semantics=("arbitrary",) * 3))(q, k, v, o, lse, do, *rider.arrays)
    return res[0], list(res[1:])


def _att_dkv_call(name, q, k, v, o, lse, do, rider):
    tq, tk = q.shape[0], k.shape[0]
    h, kvh = q.shape[1] // HEAD_DIM, k.shape[1] // HEAD_DIM
    grp = h // kvh
    bq, bk = _att_tiles(tq, tk)
    nq = tq // bq
    nr = grp * nq
    scale = HEAD_DIM ** -0.5
    grid = (kvh, tk // bk, nr)
    nx = rider.n

    def body(*refs):
        q_ref, k_ref, v_ref, o_ref, lse_ref, do_ref = refs[:6]
        dk_ref, dv_ref = refs[6 + nx:8 + nx]
        rider.run(refs[6:6 + nx], refs[8 + nx:8 + 2 * nx], refs[8 + 2 * nx:], *_grid_ends(grid))
        r = pl.program_id(2)

        @pl.when(r == 0)
        def _():
            dk_ref[...] = jnp.zeros_like(dk_ref)
            dv_ref[...] = jnp.zeros_like(dv_ref)
        q2 = _cast(q_ref[...] * (scale * LOG2E), False)
        s = _dot(q2, k_ref[...], ((1,), (1,)), False)
        p = jnp.exp2(s - lse_ref[...] * LOG2E)
        dv_ref[...] += _dot(p, do_ref[...], ((0,), (0,)), False)
        dp = _dot(do_ref[...], v_ref[...], ((1,), (1,)), False)
        delta = jnp.sum(do_ref[...] * o_ref[...], axis=1, keepdims=True)
        ds = p * (dp - delta)
        dk_ref[...] += _dot(ds, q2, ((0,), (0,)), False) * LN2

    qspec = pl.BlockSpec((bq, HEAD_DIM), lambda kh, j, r: (r % nq, kh * grp + r // nq))
    kspec = pl.BlockSpec((bk, HEAD_DIM), lambda kh, j, r: (j, kh))
    res = pl.pallas_call(
        body, name=name, grid=grid,
        in_specs=[qspec, kspec, kspec, qspec, pl.BlockSpec((None, bq, 1), lambda kh, j, r: (kh * grp + r // nq, r % nq, 0)), qspec]
                 + rider.in_specs,
        out_specs=[kspec, kspec] + rider.out_specs,
        out_shape=[jax.ShapeDtypeStruct(k.shape, F32), jax.ShapeDtypeStruct(v.shape, F32)] + rider.out_shape,
        scratch_shapes=rider.scratch,
        compiler_params=pltpu.CompilerParams(dimension_semantics=("arbitrary",) * 3))(q, k, v, o, lse, do, *rider.arrays)
    return res[0], res[1], list(res[2:])


def attention(name, q, k, v, shards=(), carry=()):
    shards, carry = list(shards), list(carry)
    half = (len(carry) + 1) // 2

    def fwd(q, k, v, shards, carry):
        o, lse, gathered = _att_fwd_call(name + "_fwd", q, k, v, _Rider(shards, 'gather'))
        return (o, gathered, carry), (q, k, v, o, lse, shards)

    def bwd(res, cts):
        q, k, v, o, lse, shards = res
        do, _, payload = cts
        dq, got_a = _att_dq_call(name + "_dq", q, k, v, o, lse, do, _Rider(payload[:half], 'a2a'))
        dk, dv, got_b = _att_dkv_call(name + "_dkv", q, k, v, o, lse, do, _Rider(payload[half:], 'a2a'))
        return dq, dk, dv, [jnp.zeros_like(a) for a in shards], got_a + got_b

    return _op(fwd, bwd)(q, k, v, shards, carry)


def loss_head(y, target):
    t, d = y.shape
    tm = min(ROW_TILE, t)

    def fwd(y, target):
        def body(y_ref, t_ref, l_ref, dy_ref):
            @pl.when(pl.program_id(0) == 0)
            def _():
                l_ref[...] = jnp.zeros_like(l_ref)
            err = y_ref[...] - t_ref[...]
            row = jnp.sum(err * err, axis=1, keepdims=True) * (1.0 / d)
            l_ref[...] += 0.5 * jnp.sum(row, axis=0, keepdims=True)
            dy_ref[...] = err * (1.0 / d)
        loss, dy = pl.pallas_call(
            body, name="loss_head", grid=(t // tm,),
            in_specs=[pl.BlockSpec((tm, d), lambda i: (i, 0)), pl.BlockSpec((tm, d), lambda i: (i, 0))],
            out_specs=[pl.BlockSpec((1, 1), lambda i: (0, 0)), pl.BlockSpec((tm, d), lambda i: (i, 0))],
            out_shape=[jax.ShapeDtypeStruct((1, 1), F32), jax.ShapeDtypeStruct((t, d), F32)],
            compiler_params=pltpu.CompilerParams(dimension_semantics=("arbitrary",)))(y, target)
        return loss[0, 0], (dy,)

    def bwd(res, g):
        return res[0] * g, jnp.zeros_like(res[0])

    return _op(fwd, bwd)(y, target)


def _my_index():
    return 4 * lax.axis_index("x") + 2 * lax.axis_index("y") + lax.axis_index("c")


def _xch_copies(ins, outs, modes, send_sems, recv_sems, local_sems):
    x, y, c = lax.axis_index("x"), lax.axis_index("y"), lax.axis_index("c")
    me = 4 * x + 2 * y + c
    copies = []
    for a in range(len(ins)):
        gather = modes[a] == 'gather'
        copies.append(pltpu.make_async_copy(ins[a] if gather else ins[a].at[me], outs[a].at[me], local_sems.at[a]))
        for dist in range(1, N_DEV):
            px = (1 - x) if dist & 4 else x
            py = (1 - y) if dist & 2 else y
            pc = (1 - c) if dist & 1 else c
            peer = 4 * px + 2 * py + pc
            copies.append(pltpu.make_async_remote_copy(
                src_ref=ins[a] if gather else ins[a].at[peer], dst_ref=outs[a].at[me],
                send_sem=send_sems.at[a * (N_DEV - 1) + dist - 1], recv_sem=recv_sems.at[a * (N_DEV - 1) + dist - 1],
                device_id=(px, py, pc), device_id_type=MESH))
    return copies


def _xch_out_shapes(arrays, modes):
    return [jax.ShapeDtypeStruct(((N_DEV,) + a.shape) if m == 'gather' else a.shape, a.dtype) for a, m in zip(arrays, modes)]


def _xch_sems(n):
    return [pltpu.SemaphoreType.DMA((n * (N_DEV - 1),)), pltpu.SemaphoreType.DMA((n * (N_DEV - 1),)), pltpu.SemaphoreType.DMA((n,))]


def exchange(name, arrays, modes):
    n = len(arrays)

    def body(*refs):
        copies = _xch_copies(refs[:n], refs[n:2 * n], modes, *refs[2 * n:])
        for cp in copies:
            cp.start()
        for cp in copies:
            cp.wait()

    hbm = pl.BlockSpec(memory_space=pl.ANY)
    return pl.pallas_call(
        body, name=name, in_specs=[hbm] * n, out_specs=[hbm] * n, out_shape=_xch_out_shapes(arrays, modes),
        scratch_shapes=_xch_sems(n), compiler_params=pltpu.CompilerParams(has_side_effects=True))(*arrays)


class _Rider:
    def __init__(self, arrays, mode):
        self.arrays, self.n, self.modes = list(arrays), len(arrays), [mode] * len(arrays)
        self.in_specs = [pl.BlockSpec(memory_space=pl.ANY)] * self.n
        self.out_specs = list(self.in_specs)
        self.out_shape = _xch_out_shapes(self.arrays, self.modes)
        self.scratch = _xch_sems(self.n) if self.n else []

    def run(self, ins, outs, sems, first, last):
        if not self.n:
            return

        @pl.when(first)
        def _():
            for cp in _xch_copies(ins, outs, self.modes, *sems):
                cp.start()

        @pl.when(last)
        def _():
            for cp in _xch_copies(ins, outs, self.modes, *sems):
                cp.wait()


def _adamw_math(w, g, m, v):
    m = ADAM_B1 * m + (1.0 - ADAM_B1) * g
    v = ADAM_B2 * v + (1.0 - ADAM_B2) * (g * g)
    m_hat = m / (1.0 - ADAM_B1 ** ADAM_STEP)
    v_hat = v / (1.0 - ADAM_B2 ** ADAM_STEP)
    delta = -ADAM_LR * (m_hat / (jnp.sqrt(v_hat) + ADAM_EPS) + ADAM_WD * w)
    return delta, m, v


def adamw(name, parts, w, m, v, scale=None, row_bytes_cap=1 << 20):
    p, r, c = parts.shape
    tr = _tile(r, max(8, row_bytes_cap // (4 * c) // 8 * 8), 8)
    extra = [] if scale is None else [scale]

    def body(p_ref, *refs):
        w_ref, m_ref, v_ref, g_ref, d_ref, nm_ref, nv_ref = refs[len(extra):]
        g = p_ref[0]
        for q in range(1, p):
            g = g + p_ref[q]
        if extra:
            g = g * refs[0][...]
        delta, nm, nv = _adamw_math(w_ref[...], g, m_ref[...], v_ref[...])
        g_ref[...] = g
        d_ref[...] = delta
        nm_ref[...] = nm
        nv_ref[...] = nv

    spec = pl.BlockSpec((tr, c), lambda i: (i, 0))
    shape = jax.ShapeDtypeStruct((r, c), F32)
    return pl.pallas_call(
        body, name=name, grid=(r // tr,),
        in_specs=[pl.BlockSpec((p, tr, c), lambda i: (0, i, 0))] + [spec] * (3 + len(extra)),
        out_specs=[spec] * 4, out_shape=[shape] * 4,
        compiler_params=pltpu.CompilerParams(dimension_semantics=("parallel",)))(parts, *extra, w, m, v)


def _rope_tables(n_lat, n_ctx):
    rows = n_lat // GRID_W
    row = jnp.repeat(jnp.arange(rows, dtype=F32), GRID_W)
    col = jnp.tile(jnp.arange(GRID_W, dtype=F32), rows)
    n_freq = HEAD_DIM // 4
    inv = ROPE_THETA ** (-jnp.arange(n_freq, dtype=F32) / n_freq)
    ang = jnp.concatenate([row[:, None] * inv, col[:, None] * inv], -1)
    cos, sin = jnp.cos(ang), jnp.sin(ang)
    cosf = jnp.concatenate([jnp.ones((n_ctx, HEAD_DIM), F32), jnp.concatenate([cos, cos], -1)], 0)
    sins = jnp.concatenate([jnp.zeros((n_ctx, HEAD_DIM), F32), jnp.concatenate([-sin, sin], -1)], 0)
    return cosf, sins


def _widths():
    rw, dw, aw, kw = RET_HEADS * HEAD_DIM, DN_HEADS * HEAD_DIM, ATT_HEADS * HEAD_DIM, ATT_KV_HEADS * HEAD_DIM
    return rw, dw, aw, kw


def _pad_in_proj(full):
    rw, dw, aw, kw = _widths()
    n_ab = 4 * DN_HEADS
    a0 = 4 * rw + 4 * dw
    main = jnp.concatenate([full[:, :a0], full[:, a0 + n_ab:]], 1)
    ab = jnp.concatenate([full[:, a0:a0 + n_ab], jnp.zeros((full.shape[0], LANE - n_ab), full.dtype)], 1)
    return jnp.concatenate([main, ab], 1)


def _unpad_in_proj(dpad):
    rw, dw, aw, kw = _widths()
    n_ab = 4 * DN_HEADS
    a0 = 4 * rw + 4 * dw
    main_w = dpad.shape[1] - LANE
    return jnp.concatenate([dpad[:, :a0], dpad[:, main_w:main_w + n_ab], dpad[:, a0:main_w]], 1)


def _chunked(a, chunk):
    t, g = a.shape
    at = a.T.reshape(g, t // chunk, chunk)
    return at[..., None], at[:, :, None, :]


def _mixer(h, wpad, like_in, sm, cosf, sins, n_ctx, tag, shards, carry):
    rw, dw, aw, kw = _widths()
    t = h.shape[0]
    nctx = n_ctx // min(ROW_TILE, t)
    rq, rk, rv, rg, dqkv, dz, aq, ak, av, ab = linear(f"in_proj{tag}", h, wpad, like_in,
                                                      split=[rw] * 4 + [3 * dw, dw, aw, kw, kw, LANE])
    big = _tile(t, 1024, 8)

    qr, kr = rowwise(f"ret_prep{tag}", _f_ret_prep, [rq, rk, cosf, sins], ['j', 'j', 'b', 'b'], [True, True, False, False],
                     [], [], [], [(HEAD_DIM, F32)] * 2, ncol=RET_HEADS, tm=big)
    logit = sm["ret_decay_logit"].reshape(2 * RET_HEADS, 1, 1)
    o_ret = chunk_scan(f"ret_scan{tag}", _f_retention, RET_CHUNK, RET_HEADS, n_ctx // RET_CHUNK, qr, kr, rv, [], [], [logit])
    y_ret, = rowwise(f"ret_out{tag}", _f_gated_out, [o_ret[0], o_ret[1], rg], ['j'] * 3, [True] * 3, [], [], [],
                     [(HEAD_DIM, F32)], ncol=RET_HEADS, tm=big)

    conv = short_conv(f"dn_conv{tag}", dqkv, sm["conv_w8"], nctx, [dw] * 3)
    dq, dk, dv = rowwise(f"dn_prep{tag}", _f_dn_prep, list(conv), ['j'] * 3, [True] * 3,
                         [], [], [], [(HEAD_DIM, F32)] * 3, ncol=DN_HEADS, tm=big)
    pad8 = lambda a: jnp.concatenate([a.reshape(1, 2 * DN_HEADS), jnp.zeros((1, LANE - 2 * DN_HEADS), F32)], 1)
    gb, = rowwise(f"dn_gates{tag}", _f_gates, [ab], ['j'], [True], [pad8(sm["dn_a_log"]), pad8(sm["dn_dt_bias"])],
                  ['shared'] * 2, [True] * 2, [(LANE, F32)], nctx=nctx)
    g_col, g_row = _chunked(gb[:, :2 * DN_HEADS], DN_CHUNK)
    b_col, _ = _chunked(gb[:, 2 * DN_HEADS:4 * DN_HEADS], DN_CHUNK)
    o_dn = chunk_scan(f"dn_scan{tag}", _f_delta, DN_CHUNK, DN_HEADS, n_ctx // DN_CHUNK, dq, dk, dv,
                      [a[:DN_HEADS] for a in (g_col, g_row, b_col)], [a[DN_HEADS:] for a in (g_col, g_row, b_col)], [])
    y_dn, = rowwise(f"dn_out{tag}", _f_gated_out_w, [o_dn[0], o_dn[1], dz], ['j'] * 3, [True] * 3,
                    [sm["dn_norm_w"].reshape(1, HEAD_DIM)], ['shared'], [True], [(HEAD_DIM, F32)], ncol=DN_HEADS, tm=big)

    qn, = rowwise(f"att_qn{tag}", _f_qk_norm, [aq, cosf, sins], ['j', 'b', 'b'], [True, False, False],
                  [sm["att_qn_w"].reshape(1, HEAD_DIM)], ['shared'], [True], [(HEAD_DIM, F32)], ncol=ATT_HEADS, tm=big)
    kn, = rowwise(f"att_kn{tag}", _f_qk_norm, [ak, cosf, sins], ['j', 'b', 'b'], [True, False, False],
                  [sm["att_kn_w"].reshape(1, HEAD_DIM)], ['shared'], [True], [(HEAD_DIM, F32)], ncol=ATT_KV_HEADS, tm=big)
    y_att_lat, gathered, carry = attention(f"att_lat{tag}", qn[n_ctx:], kn, av, shards, carry)
    return y_ret, y_dn, y_att_lat, (qn, kn, av), gathered, carry


def _layer_mix(i, depth, n_ctx, cosf, sins, w_in_pad, shards, xs, mod, like_in, sm, carry):
    d = xs.shape[1]
    tag = f"_{i}"
    nctx = n_ctx // min(ROW_TILE, xs.shape[0])
    seg = lambda kk: mod[:, kk][:, None, :]
    h, = rowwise(f"mod1{tag}", _f_mod, [xs], ['j'], [True], [seg(0), seg(1)], ['seg'] * 2, [True] * 2, [(d, F32)], nctx=nctx)
    y_ret, y_dn, y_att_lat, (qn, kn, av), gathered, carry = _mixer(h, w_in_pad, like_in, sm, cosf, sins, n_ctx, tag, shards, carry)
    if i == depth - 1:
        y = jnp.concatenate([y_ret[n_ctx:], y_dn[n_ctx:], y_att_lat], 1)
    else:
        y_att_ctx, _, _ = attention(f"att_ctx{tag}", qn[:n_ctx], kn[:n_ctx], av[:n_ctx])
        y = jnp.concatenate([y_ret, y_dn, jnp.concatenate([y_att_ctx, y_att_lat], 0)], 1)
    return (y, carry), gathered


def _layer_ffn(i, depth, n_ctx, wt, xs, y, mod, like, sm):
    alpha = (2 * depth) ** 0.25
    d = xs.shape[1]
    tag = f"_{i}"
    seg = lambda kk: mod[:, kk][:, None, :]
    lat = lambda kk: mod[1, kk][None, :]
    vec = lambda a: a.reshape(1, d)
    ln1 = [vec(sm["ln1_w"]), vec(sm["ln1_b"])]
    ln2 = [vec(sm["ln2_w"]), vec(sm["ln2_b"])]
    if i == depth - 1:
        xs = xs[n_ctx:]
        nctx = 0
        kinds = ['shared']
        g1, sh2, sc2, g2 = lat(2), lat(3), lat(4), lat(5)
    else:
        nctx = n_ctx // min(ROW_TILE, xs.shape[0])
        kinds = ['seg']
        g1, sh2, sc2, g2 = seg(2), seg(3), seg(4), seg(5)
    tt = linear(f"out_proj{tag}", y, wt["w_o"], like["w_o"])
    x1, h2 = rowwise(f"norm1{tag}", functools.partial(_f_norm_mod, alpha), [xs, tt], ['j', 'j'], [True, True],
                     [g1] + ln1 + [sh2, sc2], kinds + ['shared'] * 2 + kinds * 2, [True] * 5, [(d, F32)] * 2, nctx=nctx)
    u = linear(f"ffn_in{tag}", h2, wt["w_ffn_in"], like["w_ffn_in"])
    act = swiglu(f"swiglu{tag}", u)
    t2 = linear(f"ffn_out{tag}", act, wt["w_ffn_out"], like["w_ffn_out"])
    xs, = rowwise(f"norm2{tag}", functools.partial(_f_norm, alpha), [x1, t2], ['j', 'j'], [True, True],
                  [g2] + ln2, kinds + ['shared'] * 2, [True] * 3, [(d, F32)], nctx=nctx)
    return xs


def _ada_forward(cond16, w_ada, b_loc):
    depth, d, n = w_ada.shape
    tn = _tile(n, 512)

    def body(c_ref, w_ref, b_ref, o_ref):
        o_ref[...] = jnp.dot(c_ref[...], w_ref[...], precision=HI, preferred_element_type=F32) + b_ref[...]

    return pl.pallas_call(
        body, name="ada_fwd", grid=(depth, n // tn),
        in_specs=[pl.BlockSpec((16, d), lambda l, j: (0, 0)), pl.BlockSpec((None, d, tn), lambda l, j: (l, 0, j)),
                  pl.BlockSpec((None, 1, tn), lambda l, j: (l, 0, j))],
        out_specs=pl.BlockSpec((None, 16, tn), lambda l, j: (l, 0, j)), out_shape=jax.ShapeDtypeStruct((depth, 16, n), F32),
        compiler_params=pltpu.CompilerParams(dimension_semantics=("parallel", "parallel")))(cond16, w_ada, b_loc)


def _ada_backward(cond16, w_ada, dm16):
    depth, d, n = w_ada.shape
    tn = _tile(n, 512)

    def body(c_ref, w_ref, dm_ref, gw_ref, dc_ref):
        @pl.when(jnp.logical_and(pl.program_id(0) == 0, pl.program_id(1) == 0))
        def _():
            dc_ref[...] = jnp.zeros_like(dc_ref)
        dm = dm_ref[...]
        gw_ref[...] = lax.dot_general(c_ref[...], dm, (((0,), (0,)), ((), ())), precision=HI, preferred_element_type=F32)
        dc_ref[...] += lax.dot_general(dm, w_ref[...], (((1,), (1,)), ((), ())), precision=HI, preferred_element_type=F32)

    return pl.pallas_call(
        body, name="ada_bwd", grid=(depth, n // tn),
        in_specs=[pl.BlockSpec((16, d), lambda l, j: (0, 0)), pl.BlockSpec((None, d, tn), lambda l, j: (l, 0, j)),
                  pl.BlockSpec((None, 16, tn), lambda l, j: (l, 0, j))],
        out_specs=[pl.BlockSpec((None, d, tn), lambda l, j: (l, 0, j)), pl.BlockSpec((16, d), lambda l, j: (0, 0))],
        out_shape=[jax.ShapeDtypeStruct((depth, d, n), F32), jax.ShapeDtypeStruct((16, d), F32)],
        compiler_params=pltpu.CompilerParams(dimension_semantics=("arbitrary", "arbitrary")))(cond16, w_ada, dm16)


def _silu_rows(c_all, c_ctx):
    d = c_ctx.shape[-1]
    stacked = jnp.concatenate([c_all.reshape(N_DEV, d), c_ctx.reshape(1, d), jnp.zeros((16 - N_DEV - 1, d), F32)], 0)

    def body(c_ref, o_ref, ds_ref):
        v = c_ref[...]
        row = lax.broadcasted_iota(jnp.int32, v.shape, 0)
        o_ref[...] = jnp.where(row <= N_DEV, _silu(v), 0.0)
        cc = c_ref[N_DEV:N_DEV + 1, :]
        sg = jax.nn.sigmoid(cc)
        ds_ref[...] = sg * (1.0 + cc * (1.0 - sg))

    return pl.pallas_call(body, name="cond_silu", out_shape=[jax.ShapeDtypeStruct((16, d), F32), jax.ShapeDtypeStruct((1, d), F32)])(stacked)


def _flat_pack(arrs):
    flat = jnp.concatenate([a.reshape(-1) for a in arrs])
    n = flat.shape[0]
    rows = -(-n // LANE)
    rows = -(-rows // 8) * 8
    return jnp.concatenate([flat, jnp.zeros((rows * LANE - n,), F32)]).reshape(rows, LANE)


def _flat_unpack(packed, shapes):
    flat = packed.reshape(-1)
    out, o = [], 0
    for s in shapes:
        n = int(np.prod(s))
        out.append(flat[o:o + n].reshape(s))
        o += n
    return out


LATE = ["w_ffn_in", "w_o", "w_ffn_out"]
SMALL = ["b_ada", "ret_decay_logit", "dn_a_log", "dn_dt_bias", "dn_norm_w", "att_qn_w", "att_kn_w", "ln1_w", "ln1_b", "ln2_w", "ln2_b"]
OUT_ORDER = ['c_ctx', 'w_ada', 'b_ada', 'w_in', 'ret_decay_logit', 'dn_conv_w', 'dn_a_log', 'dn_dt_bias', 'dn_norm_w', 'att_qn_w',
             'att_kn_w', 'w_o', 'ln1_w', 'ln1_b', 'w_ffn_in', 'w_ffn_out', 'ln2_w', 'ln2_b']


def kernel(x, c, ctx, c_ctx, w_ada, b_ada, w_in, ret_decay_logit, dn_conv_w, dn_a_log, dn_dt_bias, dn_norm_w, att_qn_w, att_kn_w, w_o, ln1_w, ln1_b, w_ffn_in, w_ffn_out, ln2_w, ln2_b, loss_target, m_c_ctx, m_w_ada, m_b_ada, m_w_in, m_ret_decay_logit, m_dn_conv_w, m_dn_a_log, m_dn_dt_bias, m_dn_norm_w, m_att_qn_w, m_att_kn_w, m_w_o, m_ln1_w, m_ln1_b, m_w_ffn_in, m_w_ffn_out, m_ln2_w, m_ln2_b, v_c_ctx, v_w_ada, v_b_ada, v_w_in, v_ret_decay_logit, v_dn_conv_w, v_dn_a_log, v_dn_dt_bias, v_dn_norm_w, v_att_qn_w, v_att_kn_w, v_w_o, v_ln1_w, v_ln1_b, v_w_ffn_in, v_w_ffn_out, v_ln2_w, v_ln2_b):
    weights = dict(c_ctx=c_ctx, w_ada=w_ada, b_ada=b_ada, w_in=w_in, ret_decay_logit=ret_decay_logit, dn_conv_w=dn_conv_w,
                   dn_a_log=dn_a_log, dn_dt_bias=dn_dt_bias, dn_norm_w=dn_norm_w, att_qn_w=att_qn_w, att_kn_w=att_kn_w, w_o=w_o,
                   ln1_w=ln1_w, ln1_b=ln1_b, w_ffn_in=w_ffn_in, w_ffn_out=w_ffn_out, ln2_w=ln2_w, ln2_b=ln2_b)
    mom1 = dict(c_ctx=m_c_ctx, w_ada=m_w_ada, b_ada=m_b_ada, w_in=m_w_in, ret_decay_logit=m_ret_decay_logit, dn_conv_w=m_dn_conv_w,
                dn_a_log=m_dn_a_log, dn_dt_bias=m_dn_dt_bias, dn_norm_w=m_dn_norm_w, att_qn_w=m_att_qn_w, att_kn_w=m_att_kn_w,
                w_o=m_w_o, ln1_w=m_ln1_w, ln1_b=m_ln1_b, w_ffn_in=m_w_ffn_in, w_ffn_out=m_w_ffn_out, ln2_w=m_ln2_w, ln2_b=m_ln2_b)
    mom2 = dict(c_ctx=v_c_ctx, w_ada=v_w_ada, b_ada=v_b_ada, w_in=v_w_in, ret_decay_logit=v_ret_decay_logit, dn_conv_w=v_dn_conv_w,
                dn_a_log=v_dn_a_log, dn_dt_bias=v_dn_dt_bias, dn_norm_w=v_dn_norm_w, att_qn_w=v_att_qn_w, att_kn_w=v_att_kn_w,
                w_o=v_w_o, ln1_w=v_ln1_w, ln1_b=v_ln1_b, w_ffn_in=v_w_ffn_in, w_ffn_out=v_w_ffn_out, ln2_w=v_ln2_w, ln2_b=v_ln2_b)
    depth, d, n_ada = w_ada.shape
    me = _my_index()
    x2, ctx2, tgt2 = x[0], ctx[0], loss_target[0]

    n_ctx, n_lat = ctx2.shape[0], x2.shape[0]
    proj_w = w_in.shape[2] * N_DEV
    big = dict(w_in=w_in.astype(BF16), w_o=w_o.astype(BF16), w_ffn_in=w_ffn_in.astype(BF16), w_ffn_out=w_ffn_out.astype(BF16))

    def in_proj_slices(g_pad):
        return jnp.moveaxis(_unpad_in_proj(g_pad).reshape(d, N_DEV, -1), 1, 0)

    def ffn_slices(glike):
        g = dict(w_o=glike["w_o"].reshape(N_DEV, -1, d), w_ffn_in=glike["w_ffn_in"], w_ffn_out=glike["w_ffn_out"].reshape(N_DEV, -1, d))
        return [g[k] for k in LATE]

    c_all, conv_all, win_all = exchange("gather_first", [c, dn_conv_w, big["w_in"][0]], ['gather'] * 3)
    conv_full = jnp.moveaxis(conv_all, 0, 2).reshape(depth, DN_CONV_K, -1)
    conv_w8 = jnp.concatenate([conv_full, jnp.zeros((depth, 8 - DN_CONV_K, conv_full.shape[2]), F32)], 1)

    cond16, dsilu_ctx = _silu_rows(c_all, c_ctx)
    b_loc = lax.dynamic_slice_in_dim(b_ada, me * n_ada, n_ada, axis=1)[:, None, :]
    mod_loc = _ada_forward(cond16, w_ada, b_loc)
    mod_all, = exchange("gather_mods", [mod_loc], ['gather'])
    mod_full = jnp.moveaxis(mod_all, 0, 2).reshape(depth, 16, N_DEV * n_ada)
    mods = jnp.stack([mod_full[:, N_DEV], lax.dynamic_index_in_dim(mod_full, me, 1, keepdims=False)], 1).reshape(depth, 2, 6, d)

    sm_all = dict(ret_decay_logit=ret_decay_logit, dn_a_log=dn_a_log, dn_dt_bias=dn_dt_bias, dn_norm_w=dn_norm_w, att_qn_w=att_qn_w,
                  att_kn_w=att_kn_w, ln1_w=ln1_w, ln1_b=ln1_b, ln2_w=ln2_w, ln2_b=ln2_b, conv_w8=conv_w8)
    cosf, sins = _rope_tables(n_lat, n_ctx)
    xs = jnp.concatenate([ctx2, x2], 0)
    vjps = []
    for i in range(depth):
        more = i + 1 < depth
        sm_i = {k: v[i] for k, v in sm_all.items()}
        w_in_pad = _pad_in_proj(jnp.moveaxis(win_all, 0, 1).reshape(d, proj_w))
        like_in = jnp.zeros(w_in_pad.shape, F32)
        shards = [big[k][i] for k in LATE] + ([big["w_in"][i + 1]] if more else [])
        carry = [jnp.zeros((N_DEV,) + big[k][i].shape, F32) for k in LATE] + \
                ([jnp.zeros((N_DEV,) + big["w_in"][i + 1].shape, F32)] if more else [])
        mix = functools.partial(_layer_mix, i, depth, n_ctx, cosf, sins, w_in_pad, shards)
        (y, _), vjp_mix, gathered = jax.vjp(mix, xs, mods[i], like_in, sm_i, carry, has_aux=True)
        got_w = dict(zip(LATE, gathered))
        wt = dict(w_o=got_w["w_o"].reshape(d, d), w_ffn_in=got_w["w_ffn_in"], w_ffn_out=got_w["w_ffn_out"].reshape(-1, d))
        if more:
            win_all = gathered[len(LATE)]
        like = {k: jnp.zeros(a.shape, F32) for k, a in wt.items()}
        xs, vjp_ffn = jax.vjp(functools.partial(_layer_ffn, i, depth, n_ctx, wt), xs, y, mods[i], like, sm_i)
        vjps.append((vjp_mix, vjp_ffn))
    loss, vjp_loss = jax.vjp(lambda y: loss_head(y, tgt2), xs)
    loss = lax.psum(loss, ("x", "y", "c"))

    g, = vjp_loss(jnp.ones((), F32))
    above, late_got, in_got = None, [None] * depth, [None] * depth
    gmods, gsm = [None] * depth, [None] * depth
    for i in reversed(range(depth)):
        vjp_mix, vjp_ffn = vjps[i]
        g_xs, g_y, g_mod, glike, g_sm = vjp_ffn(g)
        payload = ffn_slices(glike) + ([above] if above is not None else [])
        g, g_mod2, g_in, g_sm2, got = vjp_mix((g_y, payload))
        g = g + g_xs
        gmods[i] = g_mod + g_mod2
        gsm[i] = {k: g_sm[k] + g_sm2[k] for k in g_sm}
        late_got[i] = got[:len(LATE)]
        if above is not None:
            in_got[i + 1] = got[len(LATE)]
        above = in_proj_slices(g_in)
    gx = g[n_ctx:]

    gm = jnp.stack(gmods).reshape(depth, 2, 6 * d)
    small_parts = {k: jnp.stack([gs[k] for gs in gsm]) for k in gsm[0]}
    small_parts["b_ada"] = gm[:, 0] + gm[:, 1]
    small_pack = _flat_pack([small_parts[k] for k in SMALL])
    g_conv = jnp.moveaxis(small_parts["conv_w8"][:, :DN_CONV_K].reshape(depth, DN_CONV_K, N_DEV, -1), 2, 0)
    gm_all, small_all, conv_got, in_got[0] = exchange("exchange_last", [gm, small_pack, g_conv, above],
                                                      ['gather', 'gather', 'a2a', 'a2a'])
    by_name = {k: jnp.stack([late_got[l][n] for l in range(depth)], 1) for n, k in enumerate(LATE)}
    by_name["w_in"] = jnp.stack(in_got, 1)

    ctx_sum = gm_all[0, :, 0]
    for p in range(1, N_DEV):
        ctx_sum = ctx_sum + gm_all[p, :, 0]
    dm_full = jnp.concatenate([jnp.moveaxis(gm_all[:, :, 1], 0, 1), ctx_sum[:, None], jnp.zeros((depth, 16 - N_DEV - 1, 6 * d), F32)], 1)
    dm16 = lax.dynamic_slice_in_dim(dm_full, me * n_ada, n_ada, axis=2)
    g_w_ada, dcond = _ada_backward(cond16, w_ada, dm16)
    dcond_all, = exchange("gather_dcond", [dcond[N_DEV:N_DEV + 1]], ['gather'])

    out = {}

    def update(name, parts3, shape, scale=None):
        w2 = weights[name].reshape(parts3.shape[1:])
        g, dl, nm, nv = adamw("adamw_" + name, parts3, w2, mom1[name].reshape(w2.shape), mom2[name].reshape(w2.shape), scale)
        out[name] = tuple(a.reshape(shape) for a in (g, dl, nm, nv))

    update("w_in", by_name["w_in"].reshape(N_DEV, depth * d, -1), w_in.shape)
    update("w_o", by_name["w_o"].reshape(N_DEV, -1, d), w_o.shape)
    update("w_ffn_in", by_name["w_ffn_in"].reshape(N_DEV, depth * d, -1), w_ffn_in.shape)
    update("w_ffn_out", by_name["w_ffn_out"].reshape(N_DEV, -1, d), w_ffn_out.shape)
    update("dn_conv_w", conv_got.reshape(N_DEV, depth * DN_CONV_K, -1), dn_conv_w.shape)
    update("w_ada", g_w_ada.reshape(1, depth * d, n_ada), w_ada.shape)
    update("c_ctx", dcond_all.reshape(N_DEV, 1, d), c_ctx.shape, scale=dsilu_ctx)

    small_shapes = [weights[k].shape for k in SMALL]
    w_pack, m_pack, v_pack = (_flat_pack([src[k] for k in SMALL]) for src in (weights, mom1, mom2))
    packs = adamw("adamw_small", small_all, w_pack, m_pack, v_pack)
    for k, vals in zip(SMALL, zip(*[_flat_unpack(pk, small_shapes) for pk in packs])):
        out[k] = vals

    res = [loss, gx[None]]
    for slot in range(4):
        res += [out[k][slot] for k in OUT_ORDER]
    return tuple(res)
```
